```python
import jax, jax.numpy as jnp
from jax import lax
import numpy as np

D_MODEL = 1024
BATCH = 4
SEQ = 8192
DEPTH = 1

PLE_DIM = 256
D_FF = 2816
SB_HEADS = 8
SB_HEAD_DIM = 64
SB_BLOCK = 128
GDN_HEADS = 8
GDN_DK = 64
GDN_DV = 64
GDN_CHUNK = 64
CONV_K = 4
SB_WIDTH = SB_HEADS * SB_HEAD_DIM
GDN_KW = GDN_HEADS * GDN_DK
GDN_VW = GDN_HEADS * GDN_DV
GDN_CONV_CH = 2 * GDN_KW + GDN_VW
SPLITS = (SB_WIDTH, SB_WIDTH, SB_WIDTH, GDN_KW, GDN_KW, GDN_VW, GDN_VW, GDN_HEADS, GDN_HEADS, D_MODEL, D_MODEL)
N_IN = 3 * SB_WIDTH + 2 * GDN_KW + 2 * GDN_VW + 2 * GDN_HEADS + 2 * D_MODEL
DEEPNORM_ALPHA = (2 * DEPTH) ** 0.25
DEEPNORM_BETA = (8 * DEPTH) ** -0.25
LN_EPS = 1e-5
RMS_EPS = 1e-6

kernel_name = "hybrid_stickbreak_gdn_macaron_deepnorm"


def layer_norm(x, g, b):
    x32 = x.astype(jnp.float32)
    mu = jnp.mean(x32, axis=-1, keepdims=True)
    xc = x32 - mu
    var = jnp.mean(xc * xc, axis=-1, keepdims=True)
    return (xc * lax.rsqrt(var + LN_EPS) * g.astype(jnp.float32) + b.astype(jnp.float32)).astype(x.dtype)


def swiglu(h, w_in, w_out):
    gate, up = jnp.split(h @ w_in, 2, axis=-1)
    return (jax.nn.silu(gate) * up) @ w_out


def causal_depthwise_conv(x, w):
    ch = x.shape[-1]
    return lax.conv_general_dilated(x, w[:, None, :].astype(x.dtype), window_strides=(1,), padding=[(CONV_K - 1, 0)], dimension_numbers=("NWC", "WIO", "NWC"), feature_group_count=ch)


def l2norm(t):
    return t * lax.rsqrt(jnp.sum(t * t, axis=-1, keepdims=True) + RMS_EPS)


def stick_breaking_attention(q, k, v):
    S = q.shape[1]
    scale = q.shape[-1] ** -0.5
    outs = []
    for blk in range(S // SB_BLOCK):
        t0 = blk * SB_BLOCK
        t1 = t0 + SB_BLOCK
        z = jnp.einsum("bqhd,bkhd->bhqk", q[:, t0:t1], k[:, :t1]).astype(jnp.float32) * scale
        qpos = t0 + jnp.arange(SB_BLOCK)[:, None]
        kpos = jnp.arange(t1)[None, :]
        mask = kpos < qpos
        log_fail = jnp.where(mask, -jax.nn.softplus(z), 0.0)
        after = lax.cumsum(log_fail, axis=3, reverse=True) - log_fail
        weights = jnp.where(mask, jnp.exp(jax.nn.log_sigmoid(z) + after), 0.0)
        outs.append(jnp.einsum("bhqk,bkhd->bqhd", weights.astype(v.dtype), v[:, :t1]))
    return jnp.concatenate(outs, axis=1)


def gated_delta_rule(q, k, v, beta, g):
    B, S, H, dk = q.shape
    dv = v.shape[-1]
    C = GDN_CHUNK
    n = S // C
    q = l2norm(q) * dk ** -0.5
    k = l2norm(k)

    def chunks(t):
        return t.reshape(B, n, C, H, -1).transpose(0, 3, 1, 2, 4)

    q, k, v = chunks(q), chunks(k), chunks(v)
    beta = beta.reshape(B, n, C, H).transpose(0, 3, 1, 2)
    g = jnp.cumsum(g.reshape(B, n, C, H).transpose(0, 3, 1, 2), axis=-1)
    causal = jnp.tril(jnp.ones((C, C), dtype=bool))
    strict = jnp.tril(jnp.ones((C, C), dtype=bool), k=-1)
    decay = jnp.exp(jnp.where(causal, g[..., :, None] - g[..., None, :], -jnp.inf))
    kk = jnp.einsum("bhncd,bhnmd->bhncm", k, k)
    lower = jnp.where(strict, beta[..., :, None] * kk * decay, 0.0)
    rhs = jnp.concatenate([v * beta[..., None], k * (beta * jnp.exp(g))[..., None]], axis=-1)
    sol = lax.linalg.triangular_solve(lower, rhs, left_side=True, lower=True, unit_diagonal=True)
    u, w = sol[..., :dv], sol[..., dv:]
    qk = jnp.where(causal, jnp.einsum("bhncd,bhnmd->bhncm", q, k) * decay, 0.0)
    g_last = g[..., -1:]
    q_dec = q * jnp.exp(g)[..., None]
    k_dec = k * jnp.exp(g_last - g)[..., None]
    chunk_decay = jnp.exp(g_last[..., 0])
    xs = (jnp.moveaxis(qk, 2, 0), jnp.moveaxis(u, 2, 0), jnp.moveaxis(w, 2, 0), jnp.moveaxis(q_dec, 2, 0), jnp.moveaxis(k_dec, 2, 0), jnp.moveaxis(chunk_decay, 2, 0))

    def step(state, inp):
        qk_c, u_c, w_c, qd_c, kd_c, dec_c = inp
        v_new = u_c - jnp.einsum("bhck,bhkv->bhcv", w_c, state)
        o = jnp.einsum("bhck,bhkv->bhcv", qd_c, state) + jnp.einsum("bhcm,bhmv->bhcv", qk_c, v_new)
        state = state * dec_c[..., None, None] + jnp.einsum("bhck,bhcv->bhkv", kd_c, v_new)
        return state, o

    state0 = jnp.zeros((B, H, dk, dv), jnp.float32)
    _, o = lax.scan(step, state0, xs)
    return o.transpose(1, 0, 3, 2, 4).reshape(B, S, H, dv)


def hybrid_mixer(h, w_in, b_gate, conv_w, a_log, dt_bias, gdn_norm_w, w_branch_sb, w_branch_gdn, w_mix_out):
    B, S, _ = h.shape
    idx = np.cumsum(np.array(SPLITS))[:-1].tolist()
    sb_q, sb_k, sb_v, gq, gk, gv, gz, gb, ga, gate_sb, gate_gdn = jnp.split(h @ w_in, idx, axis=-1)
    y_sb = stick_breaking_attention(sb_q.reshape(B, S, SB_HEADS, SB_HEAD_DIM), sb_k.reshape(B, S, SB_HEADS, SB_HEAD_DIM), sb_v.reshape(B, S, SB_HEADS, SB_HEAD_DIM))
    y_sb = y_sb.reshape(B, S, SB_WIDTH) @ w_branch_sb
    qkv = jax.nn.silu(causal_depthwise_conv(jnp.concatenate([gq, gk, gv], axis=-1), conv_w))
    cq, ck, cv = jnp.split(qkv.astype(jnp.float32), [GDN_KW, 2 * GDN_KW], axis=-1)
    beta = jax.nn.sigmoid(gb.astype(jnp.float32))
    g = -jnp.exp(a_log.astype(jnp.float32)) * jax.nn.softplus(ga.astype(jnp.float32) + dt_bias.astype(jnp.float32))
    o = gated_delta_rule(cq.reshape(B, S, GDN_HEADS, GDN_DK), ck.reshape(B, S, GDN_HEADS, GDN_DK), cv.reshape(B, S, GDN_HEADS, GDN_DV), beta, g)
    o = o * lax.rsqrt(jnp.mean(o * o, axis=-1, keepdims=True) + RMS_EPS) * gdn_norm_w.astype(jnp.float32)
    o = o * jax.nn.silu(gz.astype(jnp.float32).reshape(B, S, GDN_HEADS, GDN_DV))
    y_gdn = o.reshape(B, S, GDN_VW).astype(h.dtype) @ w_branch_gdn
    g_sb = jax.nn.sigmoid(gate_sb + b_gate[:D_MODEL])
    g_gdn = jax.nn.sigmoid(gate_gdn + b_gate[D_MODEL:])
    return (g_sb * y_sb + g_gdn * y_gdn) @ w_mix_out


def setup_inputs(seed: int = 0) -> dict:
    key = jax.random.key(seed)
    ks = jax.random.split(key, 32)
    f32 = jnp.float32

    def nrm(k, shape, scale):
        return jax.random.normal(k, shape, f32) * scale

    dt = jnp.exp(jax.random.uniform(ks[14], (DEPTH, GDN_HEADS), f32, minval=np.log(1e-3), maxval=np.log(0.1)))
    return {
        "x": nrm(ks[0], (BATCH, SEQ, D_MODEL), 1.0),
        "p": nrm(ks[1], (DEPTH, BATCH, SEQ, PLE_DIM), 1.0),
        "ffn1_w_in": nrm(ks[2], (DEPTH, D_MODEL, 2 * D_FF), D_MODEL ** -0.5),
        "ffn1_w_out": nrm(ks[3], (DEPTH, D_FF, D_MODEL), D_FF ** -0.5 * DEEPNORM_BETA),
        "ln1_g": 1.0 + nrm(ks[4], (DEPTH, D_MODEL), 0.02),
        "ln1_b": nrm(ks[5], (DEPTH, D_MODEL), 0.02),
        "w_mix_in": nrm(ks[6], (DEPTH, D_MODEL, N_IN), D_MODEL ** -0.5),
        "b_gate": nrm(ks[7], (DEPTH, 2 * D_MODEL), 0.1),
        "conv_w": nrm(ks[8], (DEPTH, CONV_K, GDN_CONV_CH), CONV_K ** -0.5),
        "a_log": jnp.log(jax.random.uniform(ks[9], (DEPTH, GDN_HEADS), f32, minval=1.0, maxval=16.0)),
        "dt_bias": dt + jnp.log(-jnp.expm1(-dt)),
        "gdn_norm_w": 1.0 + nrm(ks[10], (DEPTH, GDN_DV), 0.02),
        "w_branch_sb": nrm(ks[11], (DEPTH, SB_WIDTH, D_MODEL), SB_WIDTH ** -0.5),
        "w_branch_gdn": nrm(ks[12], (DEPTH, GDN_VW, D_MODEL), GDN_VW ** -0.5),
        "w_mix_out": nrm(ks[13], (DEPTH, D_MODEL, D_MODEL), D_MODEL ** -0.5 * DEEPNORM_BETA),
        "ln2_g": 1.0 + nrm(ks[15], (DEPTH, D_MODEL), 0.02),
        "ln2_b": nrm(ks[16], (DEPTH, D_MODEL), 0.02),
        "ffn2_w_in": nrm(ks[17], (DEPTH, D_MODEL, 2 * D_FF), D_MODEL ** -0.5),
        "ffn2_w_out": nrm(ks[18], (DEPTH, D_FF, D_MODEL), D_FF ** -0.5 * DEEPNORM_BETA),
        "ln3_g": 1.0 + nrm(ks[19], (DEPTH, D_MODEL), 0.02),
        "ln3_b": nrm(ks[20], (DEPTH, D_MODEL), 0.02),
        "w_ple_gate": nrm(ks[21], (DEPTH, D_MODEL, D_MODEL), D_MODEL ** -0.5),
        "b_ple_gate": nrm(ks[22], (DEPTH, D_MODEL), 0.1),
        "w_ple": nrm(ks[23], (DEPTH, PLE_DIM, D_MODEL), PLE_DIM ** -0.5 * DEEPNORM_BETA),
        "ln4_g": 1.0 + nrm(ks[24], (DEPTH, D_MODEL), 0.02),
        "ln4_b": nrm(ks[25], (DEPTH, D_MODEL), 0.02),
    }


def reference(x, p, ffn1_w_in, ffn1_w_out, ln1_g, ln1_b, w_mix_in, b_gate, conv_w, a_log, dt_bias, gdn_norm_w, w_branch_sb, w_branch_gdn, w_mix_out, ln2_g, ln2_b, ffn2_w_in, ffn2_w_out, ln3_g, ln3_b, w_ple_gate, b_ple_gate, w_ple, ln4_g, ln4_b):
    h = x
    for i in range(DEPTH):
        h = layer_norm(DEEPNORM_ALPHA * h + 0.5 * swiglu(h, ffn1_w_in[i], ffn1_w_out[i]), ln1_g[i], ln1_b[i])
        mix = hybrid_mixer(h, w_mix_in[i], b_gate[i], conv_w[i], a_log[i], dt_bias[i], gdn_norm_w[i], w_branch_sb[i], w_branch_gdn[i], w_mix_out[i])
        h = layer_norm(DEEPNORM_ALPHA * h + mix, ln2_g[i], ln2_b[i])
        h = layer_norm(DEEPNORM_ALPHA * h + 0.5 * swiglu(h, ffn2_w_in[i], ffn2_w_out[i]), ln3_g[i], ln3_b[i])
        ple = jax.nn.sigmoid(h @ w_ple_gate[i] + b_ple_gate[i]) * (p[i] @ w_ple[i])
        h = layer_norm(DEEPNORM_ALPHA * h + ple, ln4_g[i], ln4_b[i])
    return h
```

```python
import functools

import jax
import jax.numpy as jnp
from jax import lax
from jax.experimental import pallas as pl
from jax.experimental.pallas import tpu as pltpu

F32 = jnp.float32
BF16 = jnp.bfloat16

HEADS = 8
HEAD_DIM = 64
WIDTH = HEADS * HEAD_DIM
GDN_CHUNK = 64
CONV_K = 4
LN_EPS = 1e-5
RMS_EPS = 1e-6
LOG_WEIGHT_FLOOR = -104.0

VMEM_LIMIT = 56 * 1024 * 1024


def _const_spec(shape):
    zeros = (0,) * len(shape)
    return pl.BlockSpec(shape, lambda *_: zeros, pipeline_mode=pl.Buffered(1))


def _dot(a, b):
    return jnp.dot(a, b, preferred_element_type=F32)


def _dot_nt(a, b):
    return lax.dot_general(a, b, (((1,), (1,)), ((), ())), preferred_element_type=F32)


def _dot_tn(a, b):
    return lax.dot_general(a, b, (((0,), (0,)), ((), ())), preferred_element_type=F32)


def _split2(a):
    hi = a.astype(BF16)
    lo = (a - hi.astype(F32)).astype(BF16)
    return hi, lo


def _dot01_2(a, ones):
    hi, lo = _split2(a)
    return _dot(hi, ones) + _dot(lo, ones)


def _dot01_3(a, ones):
    p0 = a.astype(BF16)
    r1 = a - p0.astype(F32)
    p1 = r1.astype(BF16)
    p2 = (r1 - p1.astype(F32)).astype(BF16)
    return _dot(p0, ones) + _dot(p1, ones) + _dot(p2, ones)


def _ones01_dot_3(ones, a):
    p0 = a.astype(BF16)
    r1 = a - p0.astype(F32)
    p1 = r1.astype(BF16)
    p2 = (r1 - p1.astype(F32)).astype(BF16)
    return _dot(ones, p0) + _dot(ones, p1) + _dot(ones, p2)


def _sigmoid(x):
    return 1.0 / (1.0 + jnp.exp(-x))


def _silu(x):
    return x * _sigmoid(x)


def _softplus(x):
    return jnp.maximum(x, 0.0) + jnp.log(1.0 + jnp.exp(-jnp.abs(x)))


def _layer_norm(r, g, b):
    mu = jnp.mean(r, axis=-1, keepdims=True)
    xc = r - mu
    var = jnp.mean(xc * xc, axis=-1, keepdims=True)
    return xc * lax.rsqrt(var + LN_EPS) * g + b


def _swiglu(xb, win_ref, wout_ref, n_chunks):
    d_ff = wout_ref.shape[0]
    fc = d_ff // n_chunks
    acc = None
    for c in range(n_chunks):
        gate = _dot(xb, win_ref[:, c * fc:(c + 1) * fc])
        up = _dot(xb, win_ref[:, d_ff + c * fc:d_ff + (c + 1) * fc])
        act = (_silu(gate) * up).astype(BF16)
        part = _dot(act, wout_ref[c * fc:(c + 1) * fc, :])
        acc = part if acc is None else acc + part
    return acc


def _ffn_ln_kernel(x_ref, win_ref, wout_ref, g_ref, b_ref, o_ref, *, alpha, n_chunks):
    x = x_ref[...]
    y = _swiglu(x.astype(BF16), win_ref, wout_ref, n_chunks)
    o_ref[...] = _layer_norm(alpha * x + 0.5 * y, g_ref[...], b_ref[...])


def _ffn_ln_ple_kernel(x_ref, p_ref, win_ref, wout_ref, g_ref, b_ref, wpg_ref, bpg_ref, wp_ref,
                       g4_ref, b4_ref, o_ref, *, alpha, n_chunks):
    x = x_ref[...]
    y = _swiglu(x.astype(BF16), win_ref, wout_ref, n_chunks)
    h = _layer_norm(alpha * x + 0.5 * y, g_ref[...], b_ref[...])
    gate = _sigmoid(_dot(h.astype(BF16), wpg_ref[...]) + bpg_ref[...])
    ple = gate * _dot(p_ref[...].astype(BF16), wp_ref[...])
    o_ref[...] = _layer_norm(alpha * h + ple, g4_ref[...], b4_ref[...])


def _ffn_ln(x2d, w_in, w_out, g, b, *, alpha, tm, n_chunks):
    n, d = x2d.shape
    row = pl.BlockSpec((tm, d), lambda i: (i, 0))
    return pl.pallas_call(
        functools.partial(_ffn_ln_kernel, alpha=alpha, n_chunks=n_chunks),
        out_shape=jax.ShapeDtypeStruct((n, d), F32),
        grid=(n // tm,),
        in_specs=[row, _const_spec(w_in.shape), _const_spec(w_out.shape),
                  _const_spec(g.shape), _const_spec(b.shape)],
        out_specs=row,
        compiler_params=pltpu.CompilerParams(
            dimension_semantics=("parallel",), vmem_limit_bytes=VMEM_LIMIT),
        name="ffn_ln",
    )(x2d, w_in, w_out, g, b)


def _ffn_ln_ple(x2d, p2d, w_in, w_out, g, b, w_pg, b_pg, w_p, g4, b4, *, alpha, tm, n_chunks):
    n, d = x2d.shape
    row = pl.BlockSpec((tm, d), lambda i: (i, 0))
    prow = pl.BlockSpec((tm, p2d.shape[1]), lambda i: (i, 0))
    consts = (w_in, w_out, g, b, w_pg, b_pg, w_p, g4, b4)
    return pl.pallas_call(
        functools.partial(_ffn_ln_ple_kernel, alpha=alpha, n_chunks=n_chunks),
        out_shape=jax.ShapeDtypeStruct((n, d), F32),
        grid=(n // tm,),
        in_specs=[row, prow] + [_const_spec(c.shape) for c in consts],
        out_specs=row,
        compiler_params=pltpu.CompilerParams(
            dimension_semantics=("parallel",), vmem_limit_bytes=VMEM_LIMIT),
        name="ffn_ln_ple",
    )(x2d, p2d, *consts)


def _inproj_kernel(h_ref, w_ref, wb_ref, wa_ref, wbt_ref, wat_ref,
                   q_ref, k_ref, v_ref, gpre_ref, gz_ref, b_ref, a_ref, bt_ref, at_ref):
    hb = h_ref[0].astype(BF16)
    res = _dot(hb, w_ref[...])
    for h in range(HEADS):
        lo = h * HEAD_DIM
        q_ref[0, h] = res[:, lo:lo + HEAD_DIM].astype(BF16)
        k_ref[0, h] = res[:, WIDTH + lo:WIDTH + lo + HEAD_DIM].astype(BF16)
        v_ref[0, h] = res[:, 2 * WIDTH + lo:2 * WIDTH + lo + HEAD_DIM].astype(BF16)
    gpre_ref[0] = res[:, 3 * WIDTH:6 * WIDTH]
    gz_ref[0] = res[:, 6 * WIDTH:7 * WIDTH]
    b_ref[0] = _dot(hb, wb_ref[...])
    a_ref[0] = _dot(hb, wa_ref[...])
    bt_ref[0] = _dot_nt(wbt_ref[...], hb)
    at_ref[0] = _dot_nt(wat_ref[...], hb)


def _inproj(h, w_main, w_b, w_a, *, tm):
    bsz, s, d = h.shape
    w_bt, w_at = w_b.T, w_a.T
    hm = jax.ShapeDtypeStruct((bsz, HEADS, s, HEAD_DIM), BF16)
    hm_spec = pl.BlockSpec((1, HEADS, tm, HEAD_DIM), lambda b, i: (b, 0, i, 0))

    def tok(width):
        return pl.BlockSpec((1, tm, width), lambda b, i: (b, i, 0))

    tr_spec = pl.BlockSpec((1, HEADS, tm), lambda b, i: (b, 0, i))
    consts = (w_main, w_b, w_a, w_bt, w_at)
    return pl.pallas_call(
        _inproj_kernel,
        out_shape=(hm, hm, hm,
                   jax.ShapeDtypeStruct((bsz, s, 3 * WIDTH), F32),
                   jax.ShapeDtypeStruct((bsz, s, WIDTH), F32),
                   jax.ShapeDtypeStruct((bsz, s, HEADS), F32),
                   jax.ShapeDtypeStruct((bsz, s, HEADS), F32),
                   jax.ShapeDtypeStruct((bsz, HEADS, s), F32),
                   jax.ShapeDtypeStruct((bsz, HEADS, s), F32)),
        grid=(bsz, s // tm),
        in_specs=[tok(d)] + [_const_spec(c.shape) for c in consts],
        out_specs=(hm_spec, hm_spec, hm_spec, tok(3 * WIDTH), tok(WIDTH),
                   tok(HEADS), tok(HEADS), tr_spec, tr_spec),
        compiler_params=pltpu.CompilerParams(
            dimension_semantics=("parallel", "parallel"), vmem_limit_bytes=VMEM_LIMIT),
        name="mixer_inproj",
    )(h, *consts)


def _sb_kernel(q_ref, k_ref, v_ref, o_ref, *, tq, heads_per_step, scale):
    i = pl.program_id(2)
    row = lax.broadcasted_iota(jnp.int32, (tq, tq), 0)
    col = lax.broadcasted_iota(jnp.int32, (tq, tq), 1)
    causal = col < row
    suffix = jnp.where(row > col, 1.0, 0.0).astype(BF16)

    outs = []
    for hh in range(heads_per_step):
        q = q_ref[0, hh] * scale

        def tile(j, carry, acc, masked, hh=hh, q=q):
            start = pl.multiple_of(j * tq, tq)
            kb = k_ref[0, hh, pl.ds(start, tq), :]
            vb = v_ref[0, hh, pl.ds(start, tq), :]
            z = _dot_nt(q, kb)
            l1p = jnp.log(1.0 + jnp.exp(-jnp.abs(z)))
            log_fail = -jnp.maximum(z, 0.0) - l1p
            log_hit = jnp.minimum(z, 0.0) - l1p
            if masked:
                log_fail = jnp.where(causal, log_fail, 0.0)
            after = _dot01_2(log_fail, suffix) + carry
            logw = log_hit + after
            if masked:
                logw = jnp.where(causal, logw, -1e30)
            w = jnp.exp(logw)
            acc = acc + _dot(w.astype(BF16), vb)
            carry = carry + jnp.sum(log_fail, axis=1, keepdims=True)
            return carry, acc

        carry, acc = tile(i, jnp.zeros((tq, 1), F32), jnp.zeros((tq, HEAD_DIM), F32), True)

        def body(t, c, tile=tile):
            return tile(i - 1 - t, c[0], c[1], False)

        carry, acc = lax.fori_loop(0, i, body, (carry, acc))
        outs.append(acc)
    o_ref[0] = jnp.concatenate(outs, axis=1).astype(o_ref.dtype)


def _sb_attention(q, k, v, *, tq, heads_per_step):
    bsz, heads, s, dh = q.shape
    hp = heads_per_step
    kv_spec = pl.BlockSpec((1, hp, s, dh), lambda b, g, i: (b, g, 0, 0))
    return pl.pallas_call(
        functools.partial(_sb_kernel, tq=tq, heads_per_step=hp, scale=dh ** -0.5),
        out_shape=jax.ShapeDtypeStruct((bsz, s, heads * dh), BF16),
        grid=(bsz, heads // hp, s // tq),
        in_specs=[pl.BlockSpec((1, hp, tq, dh), lambda b, g, i: (b, g, i, 0)), kv_spec, kv_spec],
        out_specs=pl.BlockSpec((1, tq, hp * dh), lambda b, g, i: (b, i, g)),
        compiler_params=pltpu.CompilerParams(
            dimension_semantics=("parallel", "parallel", "arbitrary"),
            vmem_limit_bytes=VMEM_LIMIT),
        name="sb_attention",
    )(q, k, v)


def _gdn_kernel(gpre_ref, gz_ref, b_ref, a_ref, bt_ref, at_ref, convw_ref, alog_ref, dtb_ref,
                alogt_ref, dtbt_ref, normw_ref, o_ref,
                xbuf, q_s, k_s, v_s, gcol_s, bcol_s, grow_s, o_s, state_s, *, ts):
    t = pl.program_id(1)
    cs = GDN_CHUNK
    n_chunks = ts // cs
    halo = 8

    @pl.when(t == 0)
    def _():
        xbuf[0:halo, :] = jnp.zeros((halo, xbuf.shape[1]), F32)
        state_s[...] = jnp.zeros(state_s.shape, F32)

    xbuf[halo:halo + ts, :] = gpre_ref[0]
    y = None
    for j in range(CONV_K):
        off = halo - (CONV_K - 1) + j
        term = convw_ref[j:j + 1, :] * xbuf[off:off + ts, :]
        y = term if y is None else y + term
    xbuf[0:halo, :] = xbuf[ts:ts + halo, :]
    qkv = _silu(y)

    r = lax.broadcasted_iota(jnp.int32, (WIDTH, WIDTH), 0) // HEAD_DIM
    c = lax.broadcasted_iota(jnp.int32, (WIDTH, WIDTH), 1) // HEAD_DIM
    head_ones = jnp.where(r == c, 1.0, 0.0).astype(BF16)
    q = qkv[:, 0:WIDTH]
    k = qkv[:, WIDTH:2 * WIDTH]
    v = qkv[:, 2 * WIDTH:3 * WIDTH]
    q = q * lax.rsqrt(_dot01_2(q * q, head_ones) + RMS_EPS) * (HEAD_DIM ** -0.5)
    k = k * lax.rsqrt(_dot01_2(k * k, head_ones) + RMS_EPS)
    for h in range(HEADS):
        lo = h * HEAD_DIM
        q_s[h] = q[:, lo:lo + HEAD_DIM]
        k_s[h] = k[:, lo:lo + HEAD_DIM]
        v_s[h] = v[:, lo:lo + HEAD_DIM]

    bcol_s[...] = _sigmoid(b_ref[0])
    g_col = -jnp.exp(alog_ref[...]) * _softplus(a_ref[0] + dtb_ref[...])
    g_row = -jnp.exp(alogt_ref[...]) * _softplus(at_ref[0] + dtbt_ref[...])
    rr = lax.broadcasted_iota(jnp.int32, (ts, ts), 0)
    cc = lax.broadcasted_iota(jnp.int32, (ts, ts), 1)
    same = (rr // cs) == (cc // cs)
    lower_incl = jnp.where(same & (cc <= rr), 1.0, 0.0).astype(BF16)
    upper_incl = jnp.where(same & (rr <= cc), 1.0, 0.0).astype(BF16)
    gcol_s[...] = _ones01_dot_3(lower_incl, g_col)
    g_row_cum = _dot01_3(g_row, upper_incl)
    for ci in range(n_chunks):
        grow_s[ci] = g_row_cum[:, ci * cs:(ci + 1) * cs]

    ri = lax.broadcasted_iota(jnp.int32, (cs, cs), 0)
    ci_ = lax.broadcasted_iota(jnp.int32, (cs, cs), 1)
    causal = ci_ <= ri
    strict = ci_ < ri
    eye = jnp.where(ci_ == ri, 1.0, 0.0).astype(F32)

    def chunk(ci, _):
        r0 = pl.multiple_of(ci * cs, cs)
        rows = pl.ds(r0, cs)
        g_all = gcol_s[rows, :]
        beta_all = bcol_s[rows, :]
        grow_all = grow_s[ci]
        outs = []
        for h in range(HEADS):
            qh = q_s[h, rows, :]
            kh = k_s[h, rows, :]
            vh = v_s[h, rows, :]
            gc = g_all[:, h:h + 1]
            gr = grow_all[h:h + 1, :]
            beta = beta_all[:, h:h + 1]
            g_last = gc[cs - 1:cs, :]
            decay = jnp.where(causal, jnp.exp(jnp.minimum(gc - gr, 0.0)), 0.0)
            kb = kh.astype(BF16)
            kk = _dot_nt(kb, kb)
            low = jnp.where(strict, beta * kk * decay, 0.0)
            tinv = eye - low
            pw = low
            for _ in range(5):
                pwb = pw.astype(BF16)
                pw = _dot(pwb, pwb)
                tinv = tinv + _dot(tinv.astype(BF16), pw.astype(BF16))
            tb = tinv.astype(BF16)
            eg = jnp.exp(gc)
            u = _dot(tb, (vh * beta).astype(BF16))
            w = _dot(tb, (kh * (beta * eg)).astype(BF16))
            qk = jnp.where(causal, _dot_nt(qh.astype(BF16), kb) * decay, 0.0)
            q_dec = (qh * eg).astype(BF16)
            k_dec = (kh * jnp.exp(g_last - gc)).astype(BF16)
            state = state_s[h]
            sb = state.astype(BF16)
            v_new = u - _dot(w.astype(BF16), sb)
            vnb = v_new.astype(BF16)
            outs.append(_dot(q_dec, sb) + _dot(qk.astype(BF16), vnb))
            state_s[h] = state * jnp.exp(g_last) + _dot_tn(k_dec, vnb)
        o_s[rows, :] = jnp.concatenate(outs, axis=1)
        return 0

    lax.fori_loop(0, n_chunks, chunk, 0)

    o = o_s[...]
    ms = _dot01_2(o * o, head_ones) * (1.0 / HEAD_DIM)
    o = o * lax.rsqrt(ms + RMS_EPS) * normw_ref[...]
    o_ref[0] = (o * _silu(gz_ref[0])).astype(o_ref.dtype)


def _gdn(gpre, gz, gb, ga, gbt, gat, conv_w, a_log, dt_bias, norm_w, *, ts):
    bsz, s, _ = gpre.shape
    alog_c, dtb_c = a_log.reshape(1, HEADS), dt_bias.reshape(1, HEADS)
    alog_r, dtb_r = a_log.reshape(HEADS, 1), dt_bias.reshape(HEADS, 1)
    normw = jnp.tile(norm_w.reshape(1, HEAD_DIM), (1, HEADS))

    def tok(width):
        return pl.BlockSpec((1, ts, width), lambda b, i: (b, i, 0))

    tr_spec = pl.BlockSpec((1, HEADS, ts), lambda b, i: (b, 0, i))
    consts = (conv_w, alog_c, dtb_c, alog_r, dtb_r, normw)
    return pl.pallas_call(
        functools.partial(_gdn_kernel, ts=ts),
        out_shape=jax.ShapeDtypeStruct((bsz, s, WIDTH), BF16),
        grid=(bsz, s // ts),
        in_specs=[tok(3 * WIDTH), tok(WIDTH), tok(HEADS), tok(HEADS), tr_spec, tr_spec]
        + [_const_spec(c.shape) for c in consts],
        out_specs=tok(WIDTH),
        scratch_shapes=[
            pltpu.VMEM((ts + 8, 3 * WIDTH), F32),
            pltpu.VMEM((HEADS, ts, HEAD_DIM), F32),
            pltpu.VMEM((HEADS, ts, HEAD_DIM), F32),
            pltpu.VMEM((HEADS, ts, HEAD_DIM), F32),
            pltpu.VMEM((ts, HEADS), F32),
            pltpu.VMEM((ts, HEADS), F32),
            pltpu.VMEM((ts // GDN_CHUNK, HEADS, GDN_CHUNK), F32),
            pltpu.VMEM((ts, WIDTH), F32),
            pltpu.VMEM((HEADS, HEAD_DIM, HEAD_DIM), F32),
        ],
        compiler_params=pltpu.CompilerParams(
            dimension_semantics=("parallel", "arbitrary"), vmem_limit_bytes=VMEM_LIMIT),
        name="gated_deltanet",
    )(gpre, gz, gb, ga, gbt, gat, *consts)


def _mix_out_kernel(h_ref, ysb_ref, ygdn_ref, wg_ref, bg_ref, wsb_ref, wgdn_ref, wout_ref,
                    g_ref, b_ref, o_ref, *, alpha):
    h = h_ref[...]
    d = h.shape[1]
    gates = _sigmoid(_dot(h.astype(BF16), wg_ref[...]) + bg_ref[...])
    y_sb = _dot(ysb_ref[...], wsb_ref[...])
    y_gdn = _dot(ygdn_ref[...], wgdn_ref[...])
    merged = gates[:, :d] * y_sb + gates[:, d:] * y_gdn
    mix = _dot(merged.astype(BF16), wout_ref[...])
    o_ref[...] = _layer_norm(alpha * h + mix, g_ref[...], b_ref[...])


def _mix_out(h2d, ysb2d, ygdn2d, w_gate, b_gate, w_sb, w_gdn, w_out, g, b, *, alpha, tm):
    n, d = h2d.shape
    row = pl.BlockSpec((tm, d), lambda i: (i, 0))
    brow = pl.BlockSpec((tm, WIDTH), lambda i: (i, 0))
    consts = (w_gate, b_gate, w_sb, w_gdn, w_out, g, b)
    return pl.pallas_call(
        functools.partial(_mix_out_kernel, alpha=alpha),
        out_shape=jax.ShapeDtypeStruct((n, d), F32),
        grid=(n // tm,),
        in_specs=[row, brow, brow] + [_const_spec(c.shape) for c in consts],
        out_specs=row,
        compiler_params=pltpu.CompilerParams(
            dimension_semantics=("parallel",), vmem_limit_bytes=VMEM_LIMIT),
        name="mix_out_ln",
    )(h2d, ysb2d, ygdn2d, *consts)


def _pick_tile(n, want):
    t = min(want, n)
    while n % t:
        t //= 2
    return t


def kernel(x, p, ffn1_w_in, ffn1_w_out, ln1_g, ln1_b, w_mix_in, b_gate, conv_w, a_log, dt_bias, gdn_norm_w, w_branch_sb, w_branch_gdn, w_mix_out, ln2_g, ln2_b, ffn2_w_in, ffn2_w_out, ln3_g, ln3_b, w_ple_gate, b_ple_gate, w_ple, ln4_g, ln4_b):
    bsz, s, d = x.shape
    depth = ffn1_w_in.shape[0]
    n = bsz * s
    alpha = (2 * depth) ** 0.25
    d_ff = ffn1_w_out.shape[1]
    n_chunks = 2 if d_ff % 256 == 0 else 1
    tm = _pick_tile(n, 512)
    ts_proj = _pick_tile(s, 512)
    ts_gdn = _pick_tile(s, 512)
    tq = _pick_tile(s, 256)

    def row(vec):
        return vec.reshape(1, -1).astype(F32)

    h = x.reshape(n, d)
    for i in range(depth):
        h = _ffn_ln(h, ffn1_w_in[i].astype(BF16), ffn1_w_out[i].astype(BF16),
                    row(ln1_g[i]), row(ln1_b[i]), alpha=alpha, tm=tm, n_chunks=n_chunks)

        w_in = w_mix_in[i]
        c0 = 7 * WIDTH
        w_main = w_in[:, :c0].astype(BF16)
        w_b = w_in[:, c0:c0 + HEADS].astype(BF16)
        w_a = w_in[:, c0 + HEADS:c0 + 2 * HEADS].astype(BF16)
        w_gate = w_in[:, c0 + 2 * HEADS:].astype(BF16)
        sq, sk, sv, gpre, gz, gb, ga, gbt, gat = _inproj(
            h.reshape(bsz, s, d), w_main, w_b, w_a, tm=ts_proj)
        y_sb = _sb_attention(sq, sk, sv, tq=tq, heads_per_step=2)
        y_gdn = _gdn(gpre, gz, gb, ga, gbt, gat, conv_w[i].astype(F32), a_log[i].astype(F32),
                     dt_bias[i].astype(F32), gdn_norm_w[i].astype(F32), ts=ts_gdn)
        h = _mix_out(h, y_sb.reshape(n, WIDTH), y_gdn.reshape(n, WIDTH), w_gate, row(b_gate[i]),
                     w_branch_sb[i].astype(BF16), w_branch_gdn[i].astype(BF16),
                     w_mix_out[i].astype(BF16), row(ln2_g[i]), row(ln2_b[i]), alpha=alpha, tm=tm)
        h = _ffn_ln_ple(h, p[i].reshape(n, -1), ffn2_w_in[i].astype(BF16), ffn2_w_out[i].astype(BF16),
                        row(ln3_g[i]), row(ln3_b[i]), w_ple_gate[i].astype(BF16), row(b_ple_gate[i]),
                        w_ple[i].astype(BF16), row(ln4_g[i]), row(ln4_b[i]),
                        alpha=alpha, tm=tm, n_chunks=n_chunks)
    return h.reshape(bsz, s, d)
```

```python
import functools

import jax
import jax.numpy as jnp
from jax import lax
from jax.experimental import pallas as pl
from jax.experimental.pallas import tpu as pltpu

F32 = jnp.float32
BF16 = jnp.bfloat16

HEADS = 8
HEAD_DIM = 64
WIDTH = HEADS * HEAD_DIM
GDN_CHUNK = 64
CONV_K = 4
LN_EPS = 1e-5
RMS_EPS = 1e-6
LOG_WEIGHT_FLOOR = -104.0

VMEM_LIMIT = 56 * 1024 * 1024


def _const_spec(shape):
    zeros = (0,) * len(shape)
    return pl.BlockSpec(shape, lambda *_: zeros, pipeline_mode=pl.Buffered(1))


def _dot(a, b):
    return jnp.dot(a, b, preferred_element_type=F32)


def _dot_nt(a, b):
    return lax.dot_general(a, b, (((1,), (1,)), ((), ())), preferred_element_type=F32)


def _dot_tn(a, b):
    return lax.dot_general(a, b, (((0,), (0,)), ((), ())), preferred_element_type=F32)


def _split2(a):
    hi = a.astype(BF16)
    lo = (a - hi.astype(F32)).astype(BF16)
    return hi, lo


def _split3(a):
    p0 = a.astype(BF16)
    r1 = a - p0.astype(F32)
    p1 = r1.astype(BF16)
    p2 = (r1 - p1.astype(F32)).astype(BF16)
    return p0, p1, p2


def _dot01_2(a, ones):
    hi, lo = _split2(a)
    return _dot(hi, ones) + _dot(lo, ones)


def _dot01_3(a, ones):
    p0, p1, p2 = _split3(a)
    return _dot(p0, ones) + _dot(p1, ones) + _dot(p2, ones)


def _ones01_dot_3(ones, a):
    p0, p1, p2 = _split3(a)
    return _dot(ones, p0) + _dot(ones, p1) + _dot(ones, p2)


def _sigmoid(x):
    return 1.0 / (1.0 + jnp.exp(-x))


def _silu(x):
    return x * _sigmoid(x)


def _softplus(x):
    return jnp.maximum(x, 0.0) + jnp.log(1.0 + jnp.exp(-jnp.abs(x)))


def _layer_norm(r, g, b):
    mu = jnp.mean(r, axis=-1, keepdims=True)
    xc = r - mu
    var = jnp.mean(xc * xc, axis=-1, keepdims=True)
    return xc * lax.rsqrt(var + LN_EPS) * g + b


def _swiglu(xb, win_ref, wout_ref, n_chunks):
    d_ff = wout_ref.shape[0]
    fc = d_ff // n_chunks
    acc = None
    for c in range(n_chunks):
        gate = _dot(xb, win_ref[:, c * fc:(c + 1) * fc])
        up = _dot(xb, win_ref[:, d_ff + c * fc:d_ff + (c + 1) * fc])
        act = (_silu(gate) * up).astype(BF16)
        part = _dot(act, wout_ref[c * fc:(c + 1) * fc, :])
        acc = part if acc is None else acc + part
    return acc


def _ffn_ln_kernel(x_ref, win_ref, wout_ref, g_ref, b_ref, o_ref, *, alpha, n_chunks):
    x = x_ref[...]
    y = _swiglu(x.astype(BF16), win_ref, wout_ref, n_chunks)
    o_ref[...] = _layer_norm(alpha * x + 0.5 * y, g_ref[...], b_ref[...])


def _ffn_ln_ple_kernel(x_ref, p_ref, win_ref, wout_ref, g_ref, b_ref, wpg_ref, bpg_ref, wp_ref,
                       g4_ref, b4_ref, o_ref, *, alpha, n_chunks):
    x = x_ref[...]
    y = _swiglu(x.astype(BF16), win_ref, wout_ref, n_chunks)
    h = _layer_norm(alpha * x + 0.5 * y, g_ref[...], b_ref[...])
    gate = _sigmoid(_dot(h.astype(BF16), wpg_ref[...]) + bpg_ref[...])
    ple = gate * _dot(p_ref[...].astype(BF16), wp_ref[...])
    o_ref[...] = _layer_norm(alpha * h + ple, g4_ref[...], b4_ref[...])


def _ffn_ln(x2d, w_in, w_out, g, b, *, alpha, tm, n_chunks):
    n, d = x2d.shape
    row = pl.BlockSpec((tm, d), lambda i: (i, 0))
    return pl.pallas_call(
        functools.partial(_ffn_ln_kernel, alpha=alpha, n_chunks=n_chunks),
        out_shape=jax.ShapeDtypeStruct((n, d), F32),
        grid=(n // tm,),
        in_specs=[row, _const_spec(w_in.shape), _const_spec(w_out.shape),
                  _const_spec(g.shape), _const_spec(b.shape)],
        out_specs=row,
        compiler_params=pltpu.CompilerParams(
            dimension_semantics=("parallel",), vmem_limit_bytes=VMEM_LIMIT),
        name="ffn_ln",
    )(x2d, w_in, w_out, g, b)


def _ffn_ln_ple(x2d, p2d, w_in, w_out, g, b, w_pg, b_pg, w_p, g4, b4, *, alpha, tm, n_chunks):
    n, d = x2d.shape
    row = pl.BlockSpec((tm, d), lambda i: (i, 0))
    prow = pl.BlockSpec((tm, p2d.shape[1]), lambda i: (i, 0))
    consts = (w_in, w_out, g, b, w_pg, b_pg, w_p, g4, b4)
    return pl.pallas_call(
        functools.partial(_ffn_ln_ple_kernel, alpha=alpha, n_chunks=n_chunks),
        out_shape=jax.ShapeDtypeStruct((n, d), F32),
        grid=(n // tm,),
        in_specs=[row, prow] + [_const_spec(c.shape) for c in consts],
        out_specs=row,
        compiler_params=pltpu.CompilerParams(
            dimension_semantics=("parallel",), vmem_limit_bytes=VMEM_LIMIT),
        name="ffn_ln_ple",
    )(x2d, p2d, *consts)


def _inproj_kernel(h_ref, w_ref, wb_ref, wa_ref, wbt_ref, wat_ref,
                   q_ref, k_ref, v_ref, gpre_ref, gz_ref, b_ref, a_ref, bt_ref, at_ref):
    hb = h_ref[0].astype(BF16)
    res = _dot(hb, w_ref[...])
    for h in range(HEADS):
        lo = h * HEAD_DIM
        q_ref[0, h] = res[:, lo:lo + HEAD_DIM].astype(BF16)
        k_ref[0, h] = res[:, WIDTH + lo:WIDTH + lo + HEAD_DIM].astype(BF16)
        v_ref[0, h] = res[:, 2 * WIDTH + lo:2 * WIDTH + lo + HEAD_DIM].astype(BF16)
    gpre_ref[0] = res[:, 3 * WIDTH:6 * WIDTH]
    gz_ref[0] = res[:, 6 * WIDTH:7 * WIDTH]
    b_ref[0] = _dot(hb, wb_ref[...])
    a_ref[0] = _dot(hb, wa_ref[...])
    bt_ref[0] = _dot_nt(wbt_ref[...], hb)
    at_ref[0] = _dot_nt(wat_ref[...], hb)


def _inproj(h, w_main, w_b, w_a, *, tm):
    bsz, s, d = h.shape
    w_bt, w_at = w_b.T, w_a.T
    hm = jax.ShapeDtypeStruct((bsz, HEADS, s, HEAD_DIM), BF16)
    hm_spec = pl.BlockSpec((1, HEADS, tm, HEAD_DIM), lambda b, i: (b, 0, i, 0))

    def tok(width):
        return pl.BlockSpec((1, tm, width), lambda b, i: (b, i, 0))

    tr_spec = pl.BlockSpec((1, HEADS, tm), lambda b, i: (b, 0, i))
    consts = (w_main, w_b, w_a, w_bt, w_at)
    return pl.pallas_call(
        _inproj_kernel,
        out_shape=(hm, hm, hm,
                   jax.ShapeDtypeStruct((bsz, s, 3 * WIDTH), F32),
                   jax.ShapeDtypeStruct((bsz, s, WIDTH), F32),
                   jax.ShapeDtypeStruct((bsz, s, HEADS), F32),
                   jax.ShapeDtypeStruct((bsz, s, HEADS), F32),
                   jax.ShapeDtypeStruct((bsz, HEADS, s), F32),
                   jax.ShapeDtypeStruct((bsz, HEADS, s), F32)),
        grid=(bsz, s // tm),
        in_specs=[tok(d)] + [_const_spec(c.shape) for c in consts],
        out_specs=(hm_spec, hm_spec, hm_spec, tok(3 * WIDTH), tok(WIDTH),
                   tok(HEADS), tok(HEADS), tr_spec, tr_spec),
        compiler_params=pltpu.CompilerParams(
            dimension_semantics=("parallel", "parallel"), vmem_limit_bytes=VMEM_LIMIT),
        name="mixer_inproj",
    )(h, *consts)


def _sb_kernel(q_ref, k_ref, v_ref, o_ref, *, tq, heads_per_step, scale):
    i = pl.program_id(2)
    hp = heads_per_step
    row = lax.broadcasted_iota(jnp.int32, (tq, tq), 0)
    col = lax.broadcasted_iota(jnp.int32, (tq, tq), 1)
    causal = col < row
    suffix = jnp.where(row > col, 1.0, 0.0).astype(BF16)
    qs = [q_ref[0, hh] * scale for hh in range(hp)]

    def tile(j, carries, accs, masked):
        start = pl.multiple_of(j * tq, tq)
        heads = range(hp)
        kbs = [k_ref[0, hh, pl.ds(start, tq), :] for hh in heads]
        vbs = [v_ref[0, hh, pl.ds(start, tq), :] for hh in heads]
        zs = [_dot_nt(qs[hh], kbs[hh]) for hh in heads]
        l1ps = [jnp.log(1.0 + jnp.exp(-jnp.abs(z))) for z in zs]
        fails = [-jnp.maximum(z, 0.0) - l for z, l in zip(zs, l1ps)]
        hits = [jnp.minimum(z, 0.0) - l for z, l in zip(zs, l1ps)]
        if masked:
            fails = [jnp.where(causal, f, 0.0) for f in fails]
        afters = [_dot(f.astype(BF16), suffix) + c for f, c in zip(fails, carries)]
        logws = [h + a for h, a in zip(hits, afters)]
        if masked:
            logws = [jnp.where(causal, lw, -1e30) for lw in logws]
        ws = [jnp.exp(lw).astype(BF16) for lw in logws]
        accs = [acc + _dot(w, vb) for acc, w, vb in zip(accs, ws, vbs)]
        carries = [c + jnp.sum(f, axis=1, keepdims=True) for c, f in zip(carries, fails)]
        return carries, accs

    def live(carries):
        top = carries[0]
        for c in carries[1:]:
            top = jnp.maximum(top, c)
        return jnp.max(top) > LOG_WEIGHT_FLOOR

    carries, accs = tile(i, [jnp.zeros((tq, 1), F32)] * hp,
                         [jnp.zeros((tq, HEAD_DIM), F32)] * hp, True)

    def cond(state):
        j, go, _, _ = state
        return jnp.logical_and(j >= 0, go)

    def body(state):
        j, _, carries, accs = state
        carries, accs = tile(j, list(carries), list(accs), False)
        return j - 1, live(carries), tuple(carries), tuple(accs)

    _, _, _, accs = lax.while_loop(
        cond, body, (i - 1, live(carries), tuple(carries), tuple(accs)))
    o_ref[0] = jnp.concatenate(list(accs), axis=1).astype(o_ref.dtype)


def _sb_attention(q, k, v, *, tq, heads_per_step):
    bsz, heads, s, dh = q.shape
    hp = heads_per_step
    kv_spec = pl.BlockSpec((1, hp, s, dh), lambda b, g, i: (b, g, 0, 0))
    return pl.pallas_call(
        functools.partial(_sb_kernel, tq=tq, heads_per_step=hp, scale=dh ** -0.5),
        out_shape=jax.ShapeDtypeStruct((bsz, s, heads * dh), BF16),
        grid=(bsz, heads // hp, s // tq),
        in_specs=[pl.BlockSpec((1, hp, tq, dh), lambda b, g, i: (b, g, i, 0)), kv_spec, kv_spec],
        out_specs=pl.BlockSpec((1, tq, hp * dh), lambda b, g, i: (b, i, g)),
        compiler_params=pltpu.CompilerParams(
            dimension_semantics=("parallel", "parallel", "arbitrary"),
            vmem_limit_bytes=VMEM_LIMIT),
        name="sb_attention",
    )(q, k, v)


def _gdn_kernel(gpre_ref, gz_ref, b_ref, a_ref, bt_ref, at_ref, convw_ref, alog_ref, dtb_ref,
                alogt_ref, dtbt_ref, normw_ref, o_ref,
                xbuf, q_s, k_s, vk_s, gcol_s, bcol_s, grow_s, o_s, state_s, *, ts):
    t = pl.program_id(1)
    cs = GDN_CHUNK
    dh = HEAD_DIM
    n_chunks = ts // cs
    halo = 8

    @pl.when(t == 0)
    def _():
        xbuf[0:halo, :] = jnp.zeros((halo, xbuf.shape[1]), F32)
        state_s[...] = jnp.zeros(state_s.shape, F32)

    xbuf[halo:halo + ts, :] = gpre_ref[0]
    y = None
    for j in range(CONV_K):
        off = halo - (CONV_K - 1) + j
        term = convw_ref[j:j + 1, :] * xbuf[off:off + ts, :]
        y = term if y is None else y + term
    xbuf[0:halo, :] = xbuf[ts:ts + halo, :]
    qkv = _silu(y)

    r = lax.broadcasted_iota(jnp.int32, (WIDTH, WIDTH), 0) // dh
    c = lax.broadcasted_iota(jnp.int32, (WIDTH, WIDTH), 1) // dh
    head_ones = jnp.where(r == c, 1.0, 0.0).astype(BF16)
    q = qkv[:, 0:WIDTH]
    k = qkv[:, WIDTH:2 * WIDTH]
    v = qkv[:, 2 * WIDTH:3 * WIDTH]
    q = q * lax.rsqrt(_dot01_2(q * q, head_ones) + RMS_EPS) * (dh ** -0.5)
    k = k * lax.rsqrt(_dot01_2(k * k, head_ones) + RMS_EPS)
    for h in range(HEADS):
        lo = h * dh
        q_s[h] = q[:, lo:lo + dh]
        k_s[h] = k[:, lo:lo + dh]
        vk_s[h] = jnp.concatenate([v[:, lo:lo + dh], k[:, lo:lo + dh]], axis=1)

    bcol_s[...] = _sigmoid(b_ref[0])
    g_col = -jnp.exp(alog_ref[...]) * _softplus(a_ref[0] + dtb_ref[...])
    g_row = -jnp.exp(alogt_ref[...]) * _softplus(at_ref[0] + dtbt_ref[...])
    rr = lax.broadcasted_iota(jnp.int32, (ts, ts), 0)
    cc = lax.broadcasted_iota(jnp.int32, (ts, ts), 1)
    same = (rr // cs) == (cc // cs)
    lower_incl = jnp.where(same & (cc <= rr), 1.0, 0.0).astype(BF16)
    upper_incl = jnp.where(same & (rr <= cc), 1.0, 0.0).astype(BF16)
    gcol_s[...] = _ones01_dot_3(lower_incl, g_col)
    g_row_cum = _dot01_3(g_row, upper_incl)
    for ci in range(n_chunks):
        grow_s[ci] = g_row_cum[:, ci * cs:(ci + 1) * cs]

    ri = lax.broadcasted_iota(jnp.int32, (cs, cs), 0)
    ci_ = lax.broadcasted_iota(jnp.int32, (cs, cs), 1)
    causal = ci_ <= ri
    strict = ci_ < ri
    v_lanes = lax.broadcasted_iota(jnp.int32, (cs, 2 * dh), 1) < dh
    zero_rows = jnp.zeros((dh, dh), BF16)
    heads = range(HEADS)

    def chunk(ci, _):
        rows = pl.ds(pl.multiple_of(ci * cs, cs), cs)
        g_all = gcol_s[rows, :]
        beta_all = bcol_s[rows, :]
        grow_all = grow_s[ci]
        gc = [g_all[:, h:h + 1] for h in heads]
        gr = [grow_all[h:h + 1, :] for h in heads]
        beta = [beta_all[:, h:h + 1] for h in heads]
        g_last = [g[cs - 1:cs, :] for g in gc]
        eg = [jnp.exp(g) for g in gc]
        qh = [q_s[h, rows, :] for h in heads]
        kh = [k_s[h, rows, :] for h in heads]
        vk = [vk_s[h, rows, :] for h in heads]
        kb = [x.astype(BF16) for x in kh]
        qk_kk = [_dot_nt(jnp.concatenate([a, b], axis=0).astype(BF16), bb)
                 for a, b, bb in zip(qh, kh, kb)]
        decay = [jnp.where(causal, jnp.exp(jnp.minimum(a - b, 0.0)), 0.0) for a, b in zip(gc, gr)]
        low = [jnp.where(strict, bt * a[cs:] * d, 0.0).astype(BF16)
               for bt, a, d in zip(beta, qk_kk, decay)]
        qk = [(a[:cs] * d).astype(BF16) for a, d in zip(qk_kk, decay)]
        sol = [x * jnp.where(v_lanes, bt, bt * e) for x, bt, e in zip(vk, beta, eg)]
        sol = [s - _dot(p, s.astype(BF16)) for s, p in zip(sol, low)]
        pw = low
        for _ in range(5):
            pw = [_dot(p, p).astype(BF16) for p in pw]
            sol = [s + _dot(p, s.astype(BF16)) for s, p in zip(sol, pw)]
        q_dec = [(a * e).astype(BF16) for a, e in zip(qh, eg)]
        k_dec = [(a * jnp.exp(gl - g)).astype(BF16) for a, gl, g in zip(kh, g_last, gc)]
        state = [state_s[h] for h in heads]
        sb = [s.astype(BF16) for s in state]
        v_new = [s[:, :dh] - _dot(s.astype(BF16), jnp.concatenate([zero_rows, x], axis=0))
                 for s, x in zip(sol, sb)]
        vnb = [x.astype(BF16) for x in v_new]
        outs = [_dot(a, s) + _dot(b, x) for a, s, b, x in zip(q_dec, sb, qk, vnb)]
        for h in heads:
            state_s[h] = state[h] * jnp.exp(g_last[h]) + _dot_tn(k_dec[h], vnb[h])
        o_s[rows, :] = jnp.concatenate(outs, axis=1)
        return 0

    lax.fori_loop(0, n_chunks, chunk, 0)

    o = o_s[...]
    ms = _dot01_2(o * o, head_ones) * (1.0 / dh)
    o = o * lax.rsqrt(ms + RMS_EPS) * normw_ref[...]
    o_ref[0] = (o * _silu(gz_ref[0])).astype(o_ref.dtype)


def _gdn(gpre, gz, gb, ga, gbt, gat, conv_w, a_log, dt_bias, norm_w, *, ts):
    bsz, s, _ = gpre.shape
    alog_c, dtb_c = a_log.reshape(1, HEADS), dt_bias.reshape(1, HEADS)
    alog_r, dtb_r = a_log.reshape(HEADS, 1), dt_bias.reshape(HEADS, 1)
    normw = jnp.tile(norm_w.reshape(1, HEAD_DIM), (1, HEADS))

    def tok(width):
        return pl.BlockSpec((1, ts, width), lambda b, i: (b, i, 0))

    tr_spec = pl.BlockSpec((1, HEADS, ts), lambda b, i: (b, 0, i))
    consts = (conv_w, alog_c, dtb_c, alog_r, dtb_r, normw)
    return pl.pallas_call(
        functools.partial(_gdn_kernel, ts=ts),
        out_shape=jax.ShapeDtypeStruct((bsz, s, WIDTH), BF16),
        grid=(bsz, s // ts),
        in_specs=[tok(3 * WIDTH), tok(WIDTH), tok(HEADS), tok(HEADS), tr_spec, tr_spec]
        + [_const_spec(c.shape) for c in consts],
        out_specs=tok(WIDTH),
        scratch_shapes=[
            pltpu.VMEM((ts + 8, 3 * WIDTH), F32),
            pltpu.VMEM((HEADS, ts, HEAD_DIM), F32),
            pltpu.VMEM((HEADS, ts, HEAD_DIM), F32),
            pltpu.VMEM((HEADS, ts, 2 * HEAD_DIM), F32),
            pltpu.VMEM((ts, HEADS), F32),
            pltpu.VMEM((ts, HEADS), F32),
            pltpu.VMEM((ts // GDN_CHUNK, HEADS, GDN_CHUNK), F32),
            pltpu.VMEM((ts, WIDTH), F32),
            pltpu.VMEM((HEADS, HEAD_DIM, HEAD_DIM), F32),
        ],
        compiler_params=pltpu.CompilerParams(
            dimension_semantics=("parallel", "arbitrary"), vmem_limit_bytes=VMEM_LIMIT),
        name="gated_deltanet",
    )(gpre, gz, gb, ga, gbt, gat, *consts)


def _mix_out_kernel(h_ref, ysb_ref, ygdn_ref, wg_ref, bg_ref, wsb_ref, wgdn_ref, wout_ref,
                    g_ref, b_ref, o_ref, *, alpha):
    h = h_ref[...]
    d = h.shape[1]
    gates = _sigmoid(_dot(h.astype(BF16), wg_ref[...]) + bg_ref[...])
    y_sb = _dot(ysb_ref[...], wsb_ref[...])
    y_gdn = _dot(ygdn_ref[...], wgdn_ref[...])
    merged = gates[:, :d] * y_sb + gates[:, d:] * y_gdn
    mix = _dot(merged.astype(BF16), wout_ref[...])
    o_ref[...] = _layer_norm(alpha * h + mix, g_ref[...], b_ref[...])


def _mix_out(h2d, ysb2d, ygdn2d, w_gate, b_gate, w_sb, w_gdn, w_out, g, b, *, alpha, tm):
    n, d = h2d.shape
    row = pl.BlockSpec((tm, d), lambda i: (i, 0))
    brow = pl.BlockSpec((tm, WIDTH), lambda i: (i, 0))
    consts = (w_gate, b_gate, w_sb, w_gdn, w_out, g, b)
    return pl.pallas_call(
        functools.partial(_mix_out_kernel, alpha=alpha),
        out_shape=jax.ShapeDtypeStruct((n, d), F32),
        grid=(n // tm,),
        in_specs=[row, brow, brow] + [_const_spec(c.shape) for c in consts],
        out_specs=row,
        compiler_params=pltpu.CompilerParams(
            dimension_semantics=("parallel",), vmem_limit_bytes=VMEM_LIMIT),
        name="mix_out_ln",
    )(h2d, ysb2d, ygdn2d, *consts)


def _pick_tile(n, want):
    t = min(want, n)
    while n % t:
        t //= 2
    return t


def kernel(x, p, ffn1_w_in, ffn1_w_out, ln1_g, ln1_b, w_mix_in, b_gate, conv_w, a_log, dt_bias, gdn_norm_w, w_branch_sb, w_branch_gdn, w_mix_out, ln2_g, ln2_b, ffn2_w_in, ffn2_w_out, ln3_g, ln3_b, w_ple_gate, b_ple_gate, w_ple, ln4_g, ln4_b):
    bsz, s, d = x.shape
    depth = ffn1_w_in.shape[0]
    n = bsz * s
    alpha = (2 * depth) ** 0.25
    d_ff = ffn1_w_out.shape[1]
    n_chunks = 2 if d_ff % 256 == 0 else 1
    tm = _pick_tile(n, 512)
    ts_proj = _pick_tile(s, 512)
    ts_gdn = _pick_tile(s, 512)
    tq = _pick_tile(s, 128)

    def row(vec):
        return vec.reshape(1, -1).astype(F32)

    h = x.reshape(n, d)
    for i in range(depth):
        h = _ffn_ln(h, ffn1_w_in[i].astype(BF16), ffn1_w_out[i].astype(BF16),
                    row(ln1_g[i]), row(ln1_b[i]), alpha=alpha, tm=tm, n_chunks=n_chunks)

        w_in = w_mix_in[i]
        c0 = 7 * WIDTH
        w_main = w_in[:, :c0].astype(BF16)
        w_b = w_in[:, c0:c0 + HEADS].astype(BF16)
        w_a = w_in[:, c0 + HEADS:c0 + 2 * HEADS].astype(BF16)
        w_gate = w_in[:, c0 + 2 * HEADS:].astype(BF16)
        sq, sk, sv, gpre, gz, gb, ga, gbt, gat = _inproj(
            h.reshape(bsz, s, d), w_main, w_b, w_a, tm=ts_proj)
        y_sb = _sb_attention(sq, sk, sv, tq=tq, heads_per_step=4)
        y_gdn = _gdn(gpre, gz, gb, ga, gbt, gat, conv_w[i].astype(F32), a_log[i].astype(F32),
                     dt_bias[i].astype(F32), gdn_norm_w[i].astype(F32), ts=ts_gdn)
        h = _mix_out(h, y_sb.reshape(n, WIDTH), y_gdn.reshape(n, WIDTH), w_gate, row(b_gate[i]),
                     w_branch_sb[i].astype(BF16), w_branch_gdn[i].astype(BF16),
                     w_mix_out[i].astype(BF16), row(ln2_g[i]), row(ln2_b[i]), alpha=alpha, tm=tm)
        h = _ffn_ln_ple(h, p[i].reshape(n, -1), ffn2_w_in[i].astype(BF16), ffn2_w_out[i].astype(BF16),
                        row(ln3_g[i]), row(ln3_b[i]), w_ple_gate[i].astype(BF16), row(b_ple_gate[i]),
                        w_ple[i].astype(BF16), row(ln4_g[i]), row(ln4_b[i]),
                        alpha=alpha, tm=tm, n_chunks=n_chunks)
    return h.reshape(bsz, s, d)
```

```python
import functools

import jax
import jax.numpy as jnp
from jax import lax
from jax.experimental import pallas as pl
from jax.experimental.pallas import tpu as pltpu

F32 = jnp.float32
BF16 = jnp.bfloat16

HEADS = 8
HEAD_DIM = 64
WIDTH = HEADS * HEAD_DIM
GDN_CHUNK = 64
CONV_K = 4
LN_EPS = 1e-5
RMS_EPS = 1e-6
LOG_WEIGHT_FLOOR = -104.0

VMEM_LIMIT = 56 * 1024 * 1024


def _const_spec(shape):
    zeros = (0,) * len(shape)
    return pl.BlockSpec(shape, lambda *_: zeros, pipeline_mode=pl.Buffered(1))


def _dot(a, b):
    return jnp.dot(a, b, preferred_element_type=F32)


def _dot_nt(a, b):
    return lax.dot_general(a, b, (((1,), (1,)), ((), ())), preferred_element_type=F32)


def _dot_tn(a, b):
    return lax.dot_general(a, b, (((0,), (0,)), ((), ())), preferred_element_type=F32)


def _split2(a):
    hi = a.astype(BF16)
    lo = (a - hi.astype(F32)).astype(BF16)
    return hi, lo


def _split3(a):
    p0 = a.astype(BF16)
    r1 = a - p0.astype(F32)
    p1 = r1.astype(BF16)
    p2 = (r1 - p1.astype(F32)).astype(BF16)
    return p0, p1, p2


def _dot01_2(a, ones):
    hi, lo = _split2(a)
    return _dot(hi, ones) + _dot(lo, ones)


def _dot01_3(a, ones):
    p0, p1, p2 = _split3(a)
    return _dot(p0, ones) + _dot(p1, ones) + _dot(p2, ones)


def _ones01_dot_3(ones, a):
    p0, p1, p2 = _split3(a)
    return _dot(ones, p0) + _dot(ones, p1) + _dot(ones, p2)


def _sigmoid(x):
    return 1.0 / (1.0 + jnp.exp(-x))


def _silu(x):
    return x * _sigmoid(x)


def _softplus(x):
    return jnp.maximum(x, 0.0) + jnp.log(1.0 + jnp.exp(-jnp.abs(x)))


def _layer_norm(r, g, b):
    mu = jnp.mean(r, axis=-1, keepdims=True)
    xc = r - mu
    var = jnp.mean(xc * xc, axis=-1, keepdims=True)
    return xc * lax.rsqrt(var + LN_EPS) * g + b


def _swiglu(xb, win_ref, wout_ref, n_chunks):
    d_ff = wout_ref.shape[0]
    fc = d_ff // n_chunks
    acc = None
    for c in range(n_chunks):
        gate = _dot(xb, win_ref[:, c * fc:(c + 1) * fc])
        up = _dot(xb, win_ref[:, d_ff + c * fc:d_ff + (c + 1) * fc])
        act = (_silu(gate) * up).astype(BF16)
        part = _dot(act, wout_ref[c * fc:(c + 1) * fc, :])
        acc = part if acc is None else acc + part
    return acc


def _ffn_ln_kernel(x_ref, win_ref, wout_ref, g_ref, b_ref, o_ref, *, alpha, n_chunks):
    x = x_ref[...]
    y = _swiglu(x.astype(BF16), win_ref, wout_ref, n_chunks)
    o_ref[...] = _layer_norm(alpha * x + 0.5 * y, g_ref[...], b_ref[...])


def _ffn_ln_ple_kernel(x_ref, p_ref, win_ref, wout_ref, g_ref, b_ref, wpg_ref, bpg_ref, wp_ref,
                       g4_ref, b4_ref, o_ref, *, alpha, n_chunks):
    x = x_ref[...]
    y = _swiglu(x.astype(BF16), win_ref, wout_ref, n_chunks)
    h = _layer_norm(alpha * x + 0.5 * y, g_ref[...], b_ref[...])
    gate = _sigmoid(_dot(h.astype(BF16), wpg_ref[...]) + bpg_ref[...])
    ple = gate * _dot(p_ref[...].astype(BF16), wp_ref[...])
    o_ref[...] = _layer_norm(alpha * h + ple, g4_ref[...], b4_ref[...])


def _ffn_ln(x2d, w_in, w_out, g, b, *, alpha, tm, n_chunks):
    n, d = x2d.shape
    row = pl.BlockSpec((tm, d), lambda i: (i, 0))
    return pl.pallas_call(
        functools.partial(_ffn_ln_kernel, alpha=alpha, n_chunks=n_chunks),
        out_shape=jax.ShapeDtypeStruct((n, d), F32),
        grid=(n // tm,),
        in_specs=[row, _const_spec(w_in.shape), _const_spec(w_out.shape),
                  _const_spec(g.shape), _const_spec(b.shape)],
        out_specs=row,
        compiler_params=pltpu.CompilerParams(
            dimension_semantics=("parallel",), vmem_limit_bytes=VMEM_LIMIT),
        name="ffn_ln",
    )(x2d, w_in, w_out, g, b)


def _ffn_ln_ple(x2d, p2d, w_in, w_out, g, b, w_pg, b_pg, w_p, g4, b4, *, alpha, tm, n_chunks):
    n, d = x2d.shape
    row = pl.BlockSpec((tm, d), lambda i: (i, 0))
    prow = pl.BlockSpec((tm, p2d.shape[1]), lambda i: (i, 0))
    consts = (w_in, w_out, g, b, w_pg, b_pg, w_p, g4, b4)
    return pl.pallas_call(
        functools.partial(_ffn_ln_ple_kernel, alpha=alpha, n_chunks=n_chunks),
        out_shape=jax.ShapeDtypeStruct((n, d), F32),
        grid=(n // tm,),
        in_specs=[row, prow] + [_const_spec(c.shape) for c in consts],
        out_specs=row,
        compiler_params=pltpu.CompilerParams(
            dimension_semantics=("parallel",), vmem_limit_bytes=VMEM_LIMIT),
        name="ffn_ln_ple",
    )(x2d, p2d, *consts)


def _inproj_kernel(h_ref, w_ref, wb_ref, wa_ref, wbt_ref, wat_ref,
                   q_ref, kv_ref, gpre_ref, gz_ref, b_ref, a_ref, bt_ref, at_ref):
    hb = h_ref[0].astype(BF16)
    res = _dot(hb, w_ref[...])
    for h in range(HEADS):
        lo = h * HEAD_DIM
        q_ref[0, h] = res[:, lo:lo + HEAD_DIM].astype(BF16)
        kv_ref[0, h] = jnp.concatenate(
            [res[:, WIDTH + lo:WIDTH + lo + HEAD_DIM],
             res[:, 2 * WIDTH + lo:2 * WIDTH + lo + HEAD_DIM]], axis=1).astype(BF16)
    gpre_ref[0] = res[:, 3 * WIDTH:6 * WIDTH]
    gz_ref[0] = res[:, 6 * WIDTH:7 * WIDTH]
    b_ref[0] = _dot(hb, wb_ref[...])
    a_ref[0] = _dot(hb, wa_ref[...])
    bt_ref[0] = _dot_nt(wbt_ref[...], hb)
    at_ref[0] = _dot_nt(wat_ref[...], hb)


def _inproj(h, w_main, w_b, w_a, *, tm):
    bsz, s, d = h.shape
    w_bt, w_at = w_b.T, w_a.T
    def hm(width):
        return jax.ShapeDtypeStruct((bsz, HEADS, s, width), BF16)

    def hm_spec(width):
        return pl.BlockSpec((1, HEADS, tm, width), lambda b, i: (b, 0, i, 0))

    def tok(width):
        return pl.BlockSpec((1, tm, width), lambda b, i: (b, i, 0))

    tr_spec = pl.BlockSpec((1, HEADS, tm), lambda b, i: (b, 0, i))
    consts = (w_main, w_b, w_a, w_bt, w_at)
    return pl.pallas_call(
        _inproj_kernel,
        out_shape=(hm(HEAD_DIM), hm(2 * HEAD_DIM),
                   jax.ShapeDtypeStruct((bsz, s, 3 * WIDTH), F32),
                   jax.ShapeDtypeStruct((bsz, s, WIDTH), F32),
                   jax.ShapeDtypeStruct((bsz, s, HEADS), F32),
                   jax.ShapeDtypeStruct((bsz, s, HEADS), F32),
                   jax.ShapeDtypeStruct((bsz, HEADS, s), F32),
                   jax.ShapeDtypeStruct((bsz, HEADS, s), F32)),
        grid=(bsz, s // tm),
        in_specs=[tok(d)] + [_const_spec(c.shape) for c in consts],
        out_specs=(hm_spec(HEAD_DIM), hm_spec(2 * HEAD_DIM), tok(3 * WIDTH), tok(WIDTH),
                   tok(HEADS), tok(HEADS), tr_spec, tr_spec),
        compiler_params=pltpu.CompilerParams(
            dimension_semantics=("parallel", "parallel"), vmem_limit_bytes=VMEM_LIMIT),
        name="mixer_inproj",
    )(h, *consts)


def _sb_kernel(q_ref, kv_ref, o_ref, *, tq, heads_per_step, scale):
    i = pl.program_id(2)
    hp = heads_per_step
    row = lax.broadcasted_iota(jnp.int32, (tq, tq), 0)
    col = lax.broadcasted_iota(jnp.int32, (tq, tq), 1)
    causal = col < row
    suffix = jnp.where(row > col, 1.0, 0.0).astype(BF16)
    no_v = jnp.zeros((tq, HEAD_DIM), BF16)
    qs = [jnp.concatenate([q_ref[0, hh] * scale, no_v], axis=1) for hh in range(hp)]

    def tile(j, carries, accs, masked):
        start = pl.multiple_of(j * tq, tq)
        heads = range(hp)
        kvs = [kv_ref[0, hh, pl.ds(start, tq), :] for hh in heads]
        zs = [_dot_nt(qs[hh], kvs[hh]) for hh in heads]
        l1ps = [jnp.log(1.0 + jnp.exp(-jnp.abs(z))) for z in zs]
        fails = [-jnp.maximum(z, 0.0) - l for z, l in zip(zs, l1ps)]
        hits = [jnp.minimum(z, 0.0) - l for z, l in zip(zs, l1ps)]
        if masked:
            fails = [jnp.where(causal, f, 0.0) for f in fails]
        afters = [_dot(f.astype(BF16), suffix) + c for f, c in zip(fails, carries)]
        logws = [h + a for h, a in zip(hits, afters)]
        if masked:
            logws = [jnp.where(causal, lw, -1e30) for lw in logws]
        ws = [jnp.exp(lw).astype(BF16) for lw in logws]
        accs = [acc + _dot(w, kv) for acc, w, kv in zip(accs, ws, kvs)]
        carries = [c + jnp.sum(f, axis=1, keepdims=True) for c, f in zip(carries, fails)]
        return carries, accs

    def live(carries):
        top = carries[0]
        for c in carries[1:]:
            top = jnp.maximum(top, c)
        return jnp.max(top) > LOG_WEIGHT_FLOOR

    carries, accs = tile(i, [jnp.zeros((tq, 1), F32)] * hp,
                         [jnp.zeros((tq, 2 * HEAD_DIM), F32)] * hp, True)

    def cond(state):
        j, go, _, _ = state
        return jnp.logical_and(j >= 0, go)

    def body(state):
        j, _, carries, accs = state
        carries, accs = tile(j, list(carries), list(accs), False)
        return j - 1, live(carries), tuple(carries), tuple(accs)

    _, _, _, accs = lax.while_loop(
        cond, body, (i - 1, live(carries), tuple(carries), tuple(accs)))
    o_ref[0] = jnp.concatenate([a[:, HEAD_DIM:] for a in accs], axis=1).astype(o_ref.dtype)


def _sb_attention(q, kv, *, tq, heads_per_step):
    bsz, heads, s, dh = q.shape
    hp = heads_per_step
    kv_spec = pl.BlockSpec((1, hp, s, 2 * dh), lambda b, g, i: (b, g, 0, 0))
    return pl.pallas_call(
        functools.partial(_sb_kernel, tq=tq, heads_per_step=hp, scale=dh ** -0.5),
        out_shape=jax.ShapeDtypeStruct((bsz, s, heads * dh), BF16),
        grid=(bsz, heads // hp, s // tq),
        in_specs=[pl.BlockSpec((1, hp, tq, dh), lambda b, g, i: (b, g, i, 0)), kv_spec],
        out_specs=pl.BlockSpec((1, tq, hp * dh), lambda b, g, i: (b, i, g)),
        compiler_params=pltpu.CompilerParams(
            dimension_semantics=("parallel", "parallel", "arbitrary"),
            vmem_limit_bytes=VMEM_LIMIT),
        name="sb_attention",
    )(q, kv)


def _gdn_kernel(gpre_ref, gz_ref, b_ref, a_ref, bt_ref, at_ref, convw_ref, alog_ref, dtb_ref,
                alogt_ref, dtbt_ref, normw_ref, o_ref,
                xbuf, q_s, k_s, vk_s, gcol_s, bcol_s, grow_s, o_s, state_s, *, ts):
    t = pl.program_id(1)
    cs = GDN_CHUNK
    dh = HEAD_DIM
    n_chunks = ts // cs
    halo = 8

    @pl.when(t == 0)
    def _():
        xbuf[0:halo, :] = jnp.zeros((halo, xbuf.shape[1]), F32)
        state_s[...] = jnp.zeros(state_s.shape, F32)

    xbuf[halo:halo + ts, :] = gpre_ref[0]
    y = None
    for j in range(CONV_K):
        off = halo - (CONV_K - 1) + j
        term = convw_ref[j:j + 1, :] * xbuf[off:off + ts, :]
        y = term if y is None else y + term
    xbuf[0:halo, :] = xbuf[ts:ts + halo, :]
    qkv = _silu(y)

    r = lax.broadcasted_iota(jnp.int32, (WIDTH, WIDTH), 0) // dh
    c = lax.broadcasted_iota(jnp.int32, (WIDTH, WIDTH), 1) // dh
    head_ones = jnp.where(r == c, 1.0, 0.0).astype(BF16)
    q = qkv[:, 0:WIDTH]
    k = qkv[:, WIDTH:2 * WIDTH]
    v = qkv[:, 2 * WIDTH:3 * WIDTH]
    q = q * lax.rsqrt(_dot01_2(q * q, head_ones) + RMS_EPS) * (dh ** -0.5)
    k = k * lax.rsqrt(_dot01_2(k * k, head_ones) + RMS_EPS)
    for h in range(HEADS):
        lo = h * dh
        q_s[h] = q[:, lo:lo + dh]
        k_s[h] = k[:, lo:lo + dh]
        vk_s[h] = jnp.concatenate([v[:, lo:lo + dh], k[:, lo:lo + dh]], axis=1)

    bcol_s[...] = _sigmoid(b_ref[0])
    g_col = -jnp.exp(alog_ref[...]) * _softplus(a_ref[0] + dtb_ref[...])
    g_row = -jnp.exp(alogt_ref[...]) * _softplus(at_ref[0] + dtbt_ref[...])
    rr = lax.broadcasted_iota(jnp.int32, (ts, ts), 0)
    cc = lax.broadcasted_iota(jnp.int32, (ts, ts), 1)
    same = (rr // cs) == (cc // cs)
    lower_incl = jnp.where(same & (cc <= rr), 1.0, 0.0).astype(BF16)
    upper_incl = jnp.where(same & (rr <= cc), 1.0, 0.0).astype(BF16)
    gcol_s[...] = _ones01_dot_3(lower_incl, g_col)
    g_row_cum = _dot01_3(g_row, upper_incl)
    for ci in range(n_chunks):
        grow_s[ci] = g_row_cum[:, ci * cs:(ci + 1) * cs]

    ri = lax.broadcasted_iota(jnp.int32, (cs, cs), 0)
    ci_ = lax.broadcasted_iota(jnp.int32, (cs, cs), 1)
    causal = ci_ <= ri
    strict = ci_ < ri
    v_lanes = lax.broadcasted_iota(jnp.int32, (cs, 2 * dh), 1) < dh
    zero_rows = jnp.zeros((dh, dh), BF16)
    heads = range(HEADS)

    unroll = 2 if n_chunks % 2 == 0 else 1
    pairs = [(u, h) for u in range(unroll) for h in heads]

    def trip(ti, _):
        cis = [ti * unroll + u for u in range(unroll)]
        rows = [pl.ds(pl.multiple_of(ci * cs, cs), cs) for ci in cis]
        g_all = [gcol_s[r, :] for r in rows]
        beta_all = [bcol_s[r, :] for r in rows]
        grow_all = [grow_s[ci] for ci in cis]
        gc = [g_all[u][:, h:h + 1] for u, h in pairs]
        gr = [grow_all[u][h:h + 1, :] for u, h in pairs]
        beta = [beta_all[u][:, h:h + 1] for u, h in pairs]
        g_last = [g[cs - 1:cs, :] for g in gc]
        eg = [jnp.exp(g) for g in gc]
        qh = [q_s[h, rows[u], :] for u, h in pairs]
        kh = [k_s[h, rows[u], :] for u, h in pairs]
        vk = [vk_s[h, rows[u], :] for u, h in pairs]
        kb = [x.astype(BF16) for x in kh]
        qk_kk = [_dot_nt(jnp.concatenate([a, b], axis=0).astype(BF16), bb)
                 for a, b, bb in zip(qh, kh, kb)]
        decay = [jnp.where(causal, jnp.exp(jnp.minimum(a - b, 0.0)), 0.0) for a, b in zip(gc, gr)]
        low = [jnp.where(strict, bt * a[cs:] * d, 0.0).astype(BF16)
               for bt, a, d in zip(beta, qk_kk, decay)]
        qk = [(a[:cs] * d).astype(BF16) for a, d in zip(qk_kk, decay)]
        sol = [x * jnp.where(v_lanes, bt, bt * e) for x, bt, e in zip(vk, beta, eg)]
        sol = [s - _dot(p, s.astype(BF16)) for s, p in zip(sol, low)]
        pw = low
        for _ in range(5):
            pw = [_dot(p, p).astype(BF16) for p in pw]
            sol = [s + _dot(p, s.astype(BF16)) for s, p in zip(sol, pw)]
        q_dec = [(a * e).astype(BF16) for a, e in zip(qh, eg)]
        k_dec = [(a * jnp.exp(gl - g)).astype(BF16) for a, gl, g in zip(kh, g_last, gc)]
        state = [state_s[h] for h in heads]
        for u in range(unroll):
            mine = slice(u * HEADS, (u + 1) * HEADS)
            sb = [s.astype(BF16) for s in state]
            v_new = [s[:, :dh] - _dot(s.astype(BF16), jnp.concatenate([zero_rows, x], axis=0))
                     for s, x in zip(sol[mine], sb)]
            vnb = [x.astype(BF16) for x in v_new]
            outs = [_dot(a, s) + _dot(b, x)
                    for a, s, b, x in zip(q_dec[mine], sb, qk[mine], vnb)]
            state = [s * jnp.exp(gl) + _dot_tn(kd, x)
                     for s, gl, kd, x in zip(state, g_last[mine], k_dec[mine], vnb)]
            o_s[rows[u], :] = jnp.concatenate(outs, axis=1)
        for h in heads:
            state_s[h] = state[h]
        return 0

    lax.fori_loop(0, n_chunks // unroll, trip, 0)

    o = o_s[...]
    ms = _dot01_2(o * o, head_ones) * (1.0 / dh)
    o = o * lax.rsqrt(ms + RMS_EPS) * normw_ref[...]
    o_ref[0] = (o * _silu(gz_ref[0])).astype(o_ref.dtype)


def _gdn(gpre, gz, gb, ga, gbt, gat, conv_w, a_log, dt_bias, norm_w, *, ts):
    bsz, s, _ = gpre.shape
    alog_c, dtb_c = a_log.reshape(1, HEADS), dt_bias.reshape(1, HEADS)
    alog_r, dtb_r = a_log.reshape(HEADS, 1), dt_bias.reshape(HEADS, 1)
    normw = jnp.tile(norm_w.reshape(1, HEAD_DIM), (1, HEADS))

    def tok(width):
        return pl.BlockSpec((1, ts, width), lambda b, i: (b, i, 0))

    tr_spec = pl.BlockSpec((1, HEADS, ts), lambda b, i: (b, 0, i))
    consts = (conv_w, alog_c, dtb_c, alog_r, dtb_r, normw)
    return pl.pallas_call(
        functools.partial(_gdn_kernel, ts=ts),
        out_shape=jax.ShapeDtypeStruct((bsz, s, WIDTH), BF16),
        grid=(bsz, s // ts),
        in_specs=[tok(3 * WIDTH), tok(WIDTH), tok(HEADS), tok(HEADS), tr_spec, tr_spec]
        + [_const_spec(c.shape) for c in consts],
        out_specs=tok(WIDTH),
        scratch_shapes=[
            pltpu.VMEM((ts + 8, 3 * WIDTH), F32),
            pltpu.VMEM((HEADS, ts, HEAD_DIM), F32),
            pltpu.VMEM((HEADS, ts, HEAD_DIM), F32),
            pltpu.VMEM((HEADS, ts, 2 * HEAD_DIM), F32),
            pltpu.VMEM((ts, HEADS), F32),
            pltpu.VMEM((ts, HEADS), F32),
            pltpu.VMEM((ts // GDN_CHUNK, HEADS, GDN_CHUNK), F32),
            pltpu.VMEM((ts, WIDTH), F32),
            pltpu.VMEM((HEADS, HEAD_DIM, HEAD_DIM), F32),
        ],
        compiler_params=pltpu.CompilerParams(
            dimension_semantics=("parallel", "arbitrary"), vmem_limit_bytes=VMEM_LIMIT),
        name="gated_deltanet",
    )(gpre, gz, gb, ga, gbt, gat, *consts)


def _mix_out_kernel(h_ref, ysb_ref, ygdn_ref, wg_ref, bg_ref, wsb_ref, wgdn_ref, wout_ref,
                    g_ref, b_ref, o_ref, *, alpha):
    h = h_ref[...]
    d = h.shape[1]
    gates = _sigmoid(_dot(h.astype(BF16), wg_ref[...]) + bg_ref[...])
    y_sb = _dot(ysb_ref[...], wsb_ref[...])
    y_gdn = _dot(ygdn_ref[...], wgdn_ref[...])
    merged = gates[:, :d] * y_sb + gates[:, d:] * y_gdn
    mix = _dot(merged.astype(BF16), wout_ref[...])
    o_ref[...] = _layer_norm(alpha * h + mix, g_ref[...], b_ref[...])


def _mix_out(h2d, ysb2d, ygdn2d, w_gate, b_gate, w_sb, w_gdn, w_out, g, b, *, alpha, tm):
    n, d = h2d.shape
    row = pl.BlockSpec((tm, d), lambda i: (i, 0))
    brow = pl.BlockSpec((tm, WIDTH), lambda i: (i, 0))
    consts = (w_gate, b_gate, w_sb, w_gdn, w_out, g, b)
    return pl.pallas_call(
        functools.partial(_mix_out_kernel, alpha=alpha),
        out_shape=jax.ShapeDtypeStruct((n, d), F32),
        grid=(n // tm,),
        in_specs=[row, brow, brow] + [_const_spec(c.shape) for c in consts],
        out_specs=row,
        compiler_params=pltpu.CompilerParams(
            dimension_semantics=("parallel",), vmem_limit_bytes=VMEM_LIMIT),
        name="mix_out_ln",
    )(h2d, ysb2d, ygdn2d, *consts)


def _pick_tile(n, want):
    t = min(want, n)
    while n % t:
        t //= 2
    return t


def kernel(x, p, ffn1_w_in, ffn1_w_out, ln1_g, ln1_b, w_mix_in, b_gate, conv_w, a_log, dt_bias, gdn_norm_w, w_branch_sb, w_branch_gdn, w_mix_out, ln2_g, ln2_b, ffn2_w_in, ffn2_w_out, ln3_g, ln3_b, w_ple_gate, b_ple_gate, w_ple, ln4_g, ln4_b):
    bsz, s, d = x.shape
    depth = ffn1_w_in.shape[0]
    n = bsz * s
    alpha = (2 * depth) ** 0.25
    d_ff = ffn1_w_out.shape[1]
    n_chunks = 2 if d_ff % 256 == 0 else 1
    tm = _pick_tile(n, 512)
    ts_proj = _pick_tile(s, 512)
    ts_gdn = _pick_tile(s, 512)
    tq = _pick_tile(s, 128)

    def row(vec):
        return vec.reshape(1, -1).astype(F32)

    h = x.reshape(n, d)
    for i in range(depth):
        h = _ffn_ln(h, ffn1_w_in[i].astype(BF16), ffn1_w_out[i].astype(BF16),
                    row(ln1_g[i]), row(ln1_b[i]), alpha=alpha, tm=tm, n_chunks=n_chunks)

        w_in = w_mix_in[i]
        c0 = 7 * WIDTH
        w_main = w_in[:, :c0].astype(BF16)
        w_b = w_in[:, c0:c0 + HEADS].astype(BF16)
        w_a = w_in[:, c0 + HEADS:c0 + 2 * HEADS].astype(BF16)
        w_gate = w_in[:, c0 + 2 * HEADS:].astype(BF16)
        sq, skv, gpre, gz, gb, ga, gbt, gat = _inproj(
            h.reshape(bsz, s, d), w_main, w_b, w_a, tm=ts_proj)
        y_sb = _sb_attention(sq, skv, tq=tq, heads_per_step=HEADS)
        y_gdn = _gdn(gpre, gz, gb, ga, gbt, gat, conv_w[i].astype(F32), a_log[i].astype(F32),
                     dt_bias[i].astype(F32), gdn_norm_w[i].astype(F32), ts=ts_gdn)
        h = _mix_out(h, y_sb.reshape(n, WIDTH), y_gdn.reshape(n, WIDTH), w_gate, row(b_gate[i]),
                     w_branch_sb[i].astype(BF16), w_branch_gdn[i].astype(BF16),
                     w_mix_out[i].astype(BF16), row(ln2_g[i]), row(ln2_b[i]), alpha=alpha, tm=tm)
        h = _ffn_ln_ple(h, p[i].reshape(n, -1), ffn2_w_in[i].astype(BF16), ffn2_w_out[i].astype(BF16),
                        row(ln3_g[i]), row(ln3_b[i]), w_ple_gate[i].astype(BF16), row(b_ple_gate[i]),
                        w_ple[i].astype(BF16), row(ln4_g[i]), row(ln4_b[i]),
                        alpha=alpha, tm=tm, n_chunks=n_chunks)
    return h.reshape(bsz, s, d)
```

```python
import functools

import jax
import jax.numpy as jnp
from jax import lax
from jax.experimental import pallas as pl
from jax.experimental.pallas import tpu as pltpu

F32 = jnp.float32
BF16 = jnp.bfloat16

HEADS = 8
HEAD_DIM = 64
WIDTH = HEADS * HEAD_DIM
GDN_CHUNK = 64
CONV_K = 4
LN_EPS = 1e-5
RMS_EPS = 1e-6
LOG_WEIGHT_FLOOR = -104.0

VMEM_LIMIT = 56 * 1024 * 1024


def _const_spec(shape):
    zeros = (0,) * len(shape)
    return pl.BlockSpec(shape, lambda *_: zeros, pipeline_mode=pl.Buffered(1))


def _dot(a, b):
    return jnp.dot(a, b, preferred_element_type=F32)


def _dot_nt(a, b):
    return lax.dot_general(a, b, (((1,), (1,)), ((), ())), preferred_element_type=F32)


def _dot_tn(a, b):
    return lax.dot_general(a, b, (((0,), (0,)), ((), ())), preferred_element_type=F32)


def _split3(a):
    p0 = a.astype(BF16)
    r1 = a - p0.astype(F32)
    p1 = r1.astype(BF16)
    p2 = (r1 - p1.astype(F32)).astype(BF16)
    return p0, p1, p2


def _dot01_3(a, ones):
    p0, p1, p2 = _split3(a)
    return _dot(p0, ones) + _dot(p1, ones) + _dot(p2, ones)


def _ones01_dot_3(ones, a):
    p0, p1, p2 = _split3(a)
    return _dot(ones, p0) + _dot(ones, p1) + _dot(ones, p2)


def _sigmoid(x):
    return 1.0 / (1.0 + jnp.exp(-x))


def _silu(x):
    return x * _sigmoid(x)


def _softplus(x):
    return jnp.maximum(x, 0.0) + jnp.log(1.0 + jnp.exp(-jnp.abs(x)))


def _layer_norm(r, g, b):
    mu = jnp.mean(r, axis=-1, keepdims=True)
    xc = r - mu
    var = jnp.mean(xc * xc, axis=-1, keepdims=True)
    return xc * lax.rsqrt(var + LN_EPS) * g + b


def _swiglu(xb, win_ref, wout_ref, n_chunks):
    d_ff = wout_ref.shape[0]
    fc = d_ff // n_chunks
    acc = None
    for c in range(n_chunks):
        gate = _dot(xb, win_ref[:, c * fc:(c + 1) * fc])
        up = _dot(xb, win_ref[:, d_ff + c * fc:d_ff + (c + 1) * fc])
        act = (_silu(gate) * up).astype(BF16)
        part = _dot(act, wout_ref[c * fc:(c + 1) * fc, :])
        acc = part if acc is None else acc + part
    return acc


def _ffn_ln_kernel(x_ref, win_ref, wout_ref, g_ref, b_ref, o_ref, *, alpha, n_chunks):
    x = x_ref[...]
    y = _swiglu(x.astype(BF16), win_ref, wout_ref, n_chunks)
    o_ref[...] = _layer_norm(alpha * x + 0.5 * y, g_ref[...], b_ref[...])


def _ffn_ln_ple_kernel(x_ref, p_ref, win_ref, wout_ref, g_ref, b_ref, wpg_ref, bpg_ref, wp_ref,
                       g4_ref, b4_ref, o_ref, *, alpha, n_chunks):
    x = x_ref[...]
    y = _swiglu(x.astype(BF16), win_ref, wout_ref, n_chunks)
    h = _layer_norm(alpha * x + 0.5 * y, g_ref[...], b_ref[...])
    gate = _sigmoid(_dot(h.astype(BF16), wpg_ref[...]) + bpg_ref[...])
    ple = gate * _dot(p_ref[...].astype(BF16), wp_ref[...])
    o_ref[...] = _layer_norm(alpha * h + ple, g4_ref[...], b4_ref[...])


def _ffn_ln(x2d, w_in, w_out, g, b, *, alpha, tm, n_chunks):
    n, d = x2d.shape
    row = pl.BlockSpec((tm, d), lambda i: (i, 0))
    return pl.pallas_call(
        functools.partial(_ffn_ln_kernel, alpha=alpha, n_chunks=n_chunks),
        out_shape=jax.ShapeDtypeStruct((n, d), F32),
        grid=(n // tm,),
        in_specs=[row, _const_spec(w_in.shape), _const_spec(w_out.shape),
                  _const_spec(g.shape), _const_spec(b.shape)],
        out_specs=row,
        compiler_params=pltpu.CompilerParams(
            dimension_semantics=("parallel",), vmem_limit_bytes=VMEM_LIMIT),
        name="ffn_ln",
    )(x2d, w_in, w_out, g, b)


def _ffn_ln_ple(x2d, p2d, w_in, w_out, g, b, w_pg, b_pg, w_p, g4, b4, *, alpha, tm, n_chunks):
    n, d = x2d.shape
    row = pl.BlockSpec((tm, d), lambda i: (i, 0))
    prow = pl.BlockSpec((tm, p2d.shape[1]), lambda i: (i, 0))
    consts = (w_in, w_out, g, b, w_pg, b_pg, w_p, g4, b4)
    return pl.pallas_call(
        functools.partial(_ffn_ln_ple_kernel, alpha=alpha, n_chunks=n_chunks),
        out_shape=jax.ShapeDtypeStruct((n, d), F32),
        grid=(n // tm,),
        in_specs=[row, prow] + [_const_spec(c.shape) for c in consts],
        out_specs=row,
        compiler_params=pltpu.CompilerParams(
            dimension_semantics=("parallel",), vmem_limit_bytes=VMEM_LIMIT),
        name="ffn_ln_ple",
    )(x2d, p2d, *consts)


def _inproj_kernel(h_ref, w_ref, wb_ref, wa_ref, wbt_ref, wat_ref,
                   q_ref, kv_ref, gpre_ref, gz_ref, b_ref, a_ref, bt_ref, at_ref):
    hb = h_ref[0].astype(BF16)
    res = _dot(hb, w_ref[...])
    for h in range(HEADS):
        lo = h * HEAD_DIM
        q_ref[0, h] = res[:, lo:lo + HEAD_DIM].astype(BF16)
        kv_ref[0, h] = jnp.concatenate(
            [res[:, WIDTH + lo:WIDTH + lo + HEAD_DIM],
             res[:, 2 * WIDTH + lo:2 * WIDTH + lo + HEAD_DIM]], axis=1).astype(BF16)
    gpre_ref[0] = res[:, 3 * WIDTH:6 * WIDTH]
    gz_ref[0] = res[:, 6 * WIDTH:7 * WIDTH]
    b_ref[0] = _dot(hb, wb_ref[...])
    a_ref[0] = _dot(hb, wa_ref[...])
    bt_ref[0] = _dot_nt(wbt_ref[...], hb)
    at_ref[0] = _dot_nt(wat_ref[...], hb)


def _inproj(h, w_main, w_b, w_a, *, tm):
    bsz, s, d = h.shape
    w_bt, w_at = w_b.T, w_a.T
    def hm(width):
        return jax.ShapeDtypeStruct((bsz, HEADS, s, width), BF16)

    def hm_spec(width):
        return pl.BlockSpec((1, HEADS, tm, width), lambda b, i: (b, 0, i, 0))

    def tok(width):
        return pl.BlockSpec((1, tm, width), lambda b, i: (b, i, 0))

    tr_spec = pl.BlockSpec((1, HEADS, tm), lambda b, i: (b, 0, i))
    consts = (w_main, w_b, w_a, w_bt, w_at)
    return pl.pallas_call(
        _inproj_kernel,
        out_shape=(hm(HEAD_DIM), hm(2 * HEAD_DIM),
                   jax.ShapeDtypeStruct((bsz, s, 3 * WIDTH), F32),
                   jax.ShapeDtypeStruct((bsz, s, WIDTH), F32),
                   jax.ShapeDtypeStruct((bsz, s, HEADS), F32),
                   jax.ShapeDtypeStruct((bsz, s, HEADS), F32),
                   jax.ShapeDtypeStruct((bsz, HEADS, s), F32),
                   jax.ShapeDtypeStruct((bsz, HEADS, s), F32)),
        grid=(bsz, s // tm),
        in_specs=[tok(d)] + [_const_spec(c.shape) for c in consts],
        out_specs=(hm_spec(HEAD_DIM), hm_spec(2 * HEAD_DIM), tok(3 * WIDTH), tok(WIDTH),
                   tok(HEADS), tok(HEADS), tr_spec, tr_spec),
        compiler_params=pltpu.CompilerParams(
            dimension_semantics=("parallel", "parallel"), vmem_limit_bytes=VMEM_LIMIT),
        name="mixer_inproj",
    )(h, *consts)


def _sb_kernel(q_ref, kv_ref, o_ref, *, tq, heads_per_step, scale):
    i = pl.program_id(2)
    hp = heads_per_step
    row = lax.broadcasted_iota(jnp.int32, (tq, tq), 0)
    col = lax.broadcasted_iota(jnp.int32, (tq, tq), 1)
    causal = col < row
    suffix = jnp.where(row > col, 1.0, 0.0).astype(BF16)
    no_v = jnp.zeros((tq, HEAD_DIM), BF16)
    qs = [jnp.concatenate([q_ref[0, hh] * scale, no_v], axis=1) for hh in range(hp)]

    def tile(j, carries, accs, masked):
        start = pl.multiple_of(j * tq, tq)
        heads = range(hp)
        kvs = [kv_ref[0, hh, pl.ds(start, tq), :] for hh in heads]
        zs = [_dot_nt(qs[hh], kvs[hh]) for hh in heads]

        def logs(z):
            hit = jnp.minimum(z, 0.0) - jnp.log(1.0 + jnp.exp(-jnp.abs(z)))
            fail = hit - z
            if masked:
                fail = jnp.where(causal, fail, 0.0)
            return hit, fail.astype(BF16), jnp.sum(fail, axis=1, keepdims=True)

        def weights(hit, after_tile, carry):
            logw = hit + (after_tile + carry)
            if masked:
                logw = jnp.where(causal, logw, -1e30)
            return jnp.exp(logw).astype(BF16)

        hits, fails, totals = zip(*[logs(z) for z in zs])
        afters = [_dot(f, suffix) for f in fails]
        ws = [weights(h, a, c) for h, a, c in zip(hits, afters, carries)]
        accs = [acc + _dot(w, kv) for acc, w, kv in zip(accs, ws, kvs)]
        carries = [c + t for c, t in zip(carries, totals)]
        return carries, accs

    def live(carries):
        top = carries[0]
        for c in carries[1:]:
            top = jnp.maximum(top, c)
        return jnp.max(top) > LOG_WEIGHT_FLOOR

    carries, accs = tile(i, [jnp.zeros((tq, 1), F32)] * hp,
                         [jnp.zeros((tq, 2 * HEAD_DIM), F32)] * hp, True)

    def cond(state):
        j, go, _, _ = state
        return jnp.logical_and(j >= 0, go)

    def body(state):
        j, _, carries, accs = state
        carries, accs = tile(j, list(carries), list(accs), False)
        return j - 1, live(carries), tuple(carries), tuple(accs)

    _, _, _, accs = lax.while_loop(
        cond, body, (i - 1, live(carries), tuple(carries), tuple(accs)))
    o_ref[0] = jnp.concatenate([a[:, HEAD_DIM:] for a in accs], axis=1).astype(o_ref.dtype)


def _sb_attention(q, kv, *, tq, heads_per_step):
    bsz, heads, s, dh = q.shape
    hp = heads_per_step
    kv_spec = pl.BlockSpec((1, hp, s, 2 * dh), lambda b, g, i: (b, g, 0, 0))
    return pl.pallas_call(
        functools.partial(_sb_kernel, tq=tq, heads_per_step=hp, scale=dh ** -0.5),
        out_shape=jax.ShapeDtypeStruct((bsz, s, heads * dh), BF16),
        grid=(bsz, heads // hp, s // tq),
        in_specs=[pl.BlockSpec((1, hp, tq, dh), lambda b, g, i: (b, g, i, 0)), kv_spec],
        out_specs=pl.BlockSpec((1, tq, hp * dh), lambda b, g, i: (b, i, g)),
        compiler_params=pltpu.CompilerParams(
            dimension_semantics=("parallel", "parallel", "arbitrary"),
            vmem_limit_bytes=VMEM_LIMIT),
        name="sb_attention",
    )(q, kv)


def _gdn_kernel(gpre_ref, gz_ref, b_ref, a_ref, bt_ref, at_ref, convw_ref, alog_ref, dtb_ref,
                alogt_ref, dtbt_ref, normw_ref, headones_ref, lower_ref, upper_ref, o_ref,
                xbuf, q_s, k_s, vk_s, gcol_s, bcol_s, grow_s, o_s, state_s, *, ts):
    t = pl.program_id(1)
    cs = GDN_CHUNK
    dh = HEAD_DIM
    n_chunks = ts // cs
    halo = 8

    @pl.when(t == 0)
    def _():
        xbuf[0:halo, :] = jnp.zeros((halo, xbuf.shape[1]), F32)
        state_s[...] = jnp.zeros(state_s.shape, F32)

    xbuf[halo:halo + ts, :] = gpre_ref[0]
    y = None
    for j in range(CONV_K):
        off = halo - (CONV_K - 1) + j
        term = convw_ref[j:j + 1, :] * xbuf[off:off + ts, :]
        y = term if y is None else y + term
    xbuf[0:halo, :] = xbuf[ts:ts + halo, :]
    qkv = _silu(y)

    head_ones = headones_ref[...]
    q = qkv[:, 0:WIDTH]
    k = qkv[:, WIDTH:2 * WIDTH]
    v = qkv[:, 2 * WIDTH:3 * WIDTH]
    q = q * (lax.rsqrt(_dot((q * q).astype(BF16), head_ones) + RMS_EPS) * (dh ** -0.5))
    k = k * lax.rsqrt(_dot((k * k).astype(BF16), head_ones) + RMS_EPS)
    for h in range(HEADS):
        lo = h * dh
        q_s[h] = q[:, lo:lo + dh]
        k_s[h] = k[:, lo:lo + dh]
        vk_s[h] = jnp.concatenate([v[:, lo:lo + dh], k[:, lo:lo + dh]], axis=1)

    bcol_s[...] = _sigmoid(b_ref[0])
    g_col = -jnp.exp(alog_ref[...]) * _softplus(a_ref[0] + dtb_ref[...])
    g_row = -jnp.exp(alogt_ref[...]) * _softplus(at_ref[0] + dtbt_ref[...])
    gcol_s[...] = _ones01_dot_3(lower_ref[...], g_col)
    g_row_cum = _dot01_3(g_row, upper_ref[...])
    for ci in range(n_chunks):
        grow_s[ci] = g_row_cum[:, ci * cs:(ci + 1) * cs]

    ri = lax.broadcasted_iota(jnp.int32, (cs, cs), 0)
    ci_ = lax.broadcasted_iota(jnp.int32, (cs, cs), 1)
    causal = ci_ <= ri
    strict = ci_ < ri
    v_lanes = lax.broadcasted_iota(jnp.int32, (cs, 2 * dh), 1) < dh
    zero_rows = jnp.zeros((dh, dh), BF16)
    heads = range(HEADS)

    unroll = 4 if n_chunks % 4 == 0 else 1
    pairs = [(u, h) for u in range(unroll) for h in heads]

    def trip(ti, _):
        cis = [ti * unroll + u for u in range(unroll)]
        rows = [pl.ds(pl.multiple_of(ci * cs, cs), cs) for ci in cis]
        g_all = [gcol_s[r, :] for r in rows]
        beta_all = [bcol_s[r, :] for r in rows]
        grow_all = [grow_s[ci] for ci in cis]
        gc = [g_all[u][:, h:h + 1] for u, h in pairs]
        gr = [grow_all[u][h:h + 1, :] for u, h in pairs]
        beta = [beta_all[u][:, h:h + 1] for u, h in pairs]
        g_last = [g[cs - 1:cs, :] for g in gc]
        eg = [jnp.exp(g) for g in gc]
        qh = [q_s[h, rows[u], :] for u, h in pairs]
        kh = [k_s[h, rows[u], :] for u, h in pairs]
        vk = [vk_s[h, rows[u], :] for u, h in pairs]
        kb = [x.astype(BF16) for x in kh]
        qk_kk = [_dot_nt(jnp.concatenate([a, b], axis=0).astype(BF16), bb)
                 for a, b, bb in zip(qh, kh, kb)]
        decay = [jnp.where(causal, jnp.exp(jnp.minimum(a - b, 0.0)), 0.0) for a, b in zip(gc, gr)]
        low = [jnp.where(strict, bt * a[cs:] * d, 0.0).astype(BF16)
               for bt, a, d in zip(beta, qk_kk, decay)]
        qk = [(a[:cs] * d).astype(BF16) for a, d in zip(qk_kk, decay)]
        sol = [x * jnp.where(v_lanes, bt, bt * e) for x, bt, e in zip(vk, beta, eg)]
        sol = [s - _dot(p, s.astype(BF16)) for s, p in zip(sol, low)]
        pw = low
        for _ in range(5):
            pw = [_dot(p, p).astype(BF16) for p in pw]
            sol = [s + _dot(p, s.astype(BF16)) for s, p in zip(sol, pw)]
        q_dec = [(a * e).astype(BF16) for a, e in zip(qh, eg)]
        k_dec = [(a * jnp.exp(gl - g)).astype(BF16) for a, gl, g in zip(kh, g_last, gc)]
        state = [state_s[h] for h in heads]
        for u in range(unroll):
            mine = slice(u * HEADS, (u + 1) * HEADS)
            sb = [s.astype(BF16) for s in state]
            v_new = [s[:, :dh] - _dot(s.astype(BF16), jnp.concatenate([zero_rows, x], axis=0))
                     for s, x in zip(sol[mine], sb)]
            vnb = [x.astype(BF16) for x in v_new]
            outs = [_dot(a, s) + _dot(b, x)
                    for a, s, b, x in zip(q_dec[mine], sb, qk[mine], vnb)]
            state = [s * jnp.exp(gl) + _dot_tn(kd, x)
                     for s, gl, kd, x in zip(state, g_last[mine], k_dec[mine], vnb)]
            o_s[rows[u], :] = jnp.concatenate(outs, axis=1)
        for h in heads:
            state_s[h] = state[h]
        return 0

    lax.fori_loop(0, n_chunks // unroll, trip, 0)

    o = o_s[...]
    ms = _dot((o * o).astype(BF16), head_ones) * (1.0 / dh)
    o = o * lax.rsqrt(ms + RMS_EPS) * normw_ref[...]
    o_ref[0] = (o * _silu(gz_ref[0])).astype(o_ref.dtype)


def _gdn(gpre, gz, gb, ga, gbt, gat, conv_w, a_log, dt_bias, norm_w, *, ts):
    bsz, s, _ = gpre.shape
    alog_c, dtb_c = a_log.reshape(1, HEADS), dt_bias.reshape(1, HEADS)
    alog_r, dtb_r = a_log.reshape(HEADS, 1), dt_bias.reshape(HEADS, 1)
    normw = jnp.tile(norm_w.reshape(1, HEAD_DIM), (1, HEADS))

    def tok(width):
        return pl.BlockSpec((1, ts, width), lambda b, i: (b, i, 0))

    tr_spec = pl.BlockSpec((1, HEADS, ts), lambda b, i: (b, 0, i))
    lane_head = jnp.arange(WIDTH) // HEAD_DIM
    head_ones = (lane_head[:, None] == lane_head[None, :]).astype(BF16)
    pos = jnp.arange(ts)
    same_chunk = (pos[:, None] // GDN_CHUNK) == (pos[None, :] // GDN_CHUNK)
    lower_incl = (same_chunk & (pos[None, :] <= pos[:, None])).astype(BF16)
    consts = (conv_w, alog_c, dtb_c, alog_r, dtb_r, normw, head_ones, lower_incl, lower_incl.T)
    return pl.pallas_call(
        functools.partial(_gdn_kernel, ts=ts),
        out_shape=jax.ShapeDtypeStruct((bsz, s, WIDTH), BF16),
        grid=(bsz, s // ts),
        in_specs=[tok(3 * WIDTH), tok(WIDTH), tok(HEADS), tok(HEADS), tr_spec, tr_spec]
        + [_const_spec(c.shape) for c in consts],
        out_specs=tok(WIDTH),
        scratch_shapes=[
            pltpu.VMEM((ts + 8, 3 * WIDTH), F32),
            pltpu.VMEM((HEADS, ts, HEAD_DIM), F32),
            pltpu.VMEM((HEADS, ts, HEAD_DIM), F32),
            pltpu.VMEM((HEADS, ts, 2 * HEAD_DIM), F32),
            pltpu.VMEM((ts, HEADS), F32),
            pltpu.VMEM((ts, HEADS), F32),
            pltpu.VMEM((ts // GDN_CHUNK, HEADS, GDN_CHUNK), F32),
            pltpu.VMEM((ts, WIDTH), F32),
            pltpu.VMEM((HEADS, HEAD_DIM, HEAD_DIM), F32),
        ],
        compiler_params=pltpu.CompilerParams(
            dimension_semantics=("parallel", "arbitrary"), vmem_limit_bytes=VMEM_LIMIT),
        name="gated_deltanet",
    )(gpre, gz, gb, ga, gbt, gat, *consts)


def _mix_out_kernel(h_ref, ysb_ref, ygdn_ref, wg_ref, bg_ref, wsb_ref, wgdn_ref, wout_ref,
                    g_ref, b_ref, o_ref, *, alpha):
    h = h_ref[...]
    d = h.shape[1]
    gates = _sigmoid(_dot(h.astype(BF16), wg_ref[...]) + bg_ref[...])
    y_sb = _dot(ysb_ref[...], wsb_ref[...])
    y_gdn = _dot(ygdn_ref[...], wgdn_ref[...])
    merged = gates[:, :d] * y_sb + gates[:, d:] * y_gdn
    mix = _dot(merged.astype(BF16), wout_ref[...])
    o_ref[...] = _layer_norm(alpha * h + mix, g_ref[...], b_ref[...])


def _mix_out(h2d, ysb2d, ygdn2d, w_gate, b_gate, w_sb, w_gdn, w_out, g, b, *, alpha, tm):
    n, d = h2d.shape
    row = pl.BlockSpec((tm, d), lambda i: (i, 0))
    brow = pl.BlockSpec((tm, WIDTH), lambda i: (i, 0))
    consts = (w_gate, b_gate, w_sb, w_gdn, w_out, g, b)
    return pl.pallas_call(
        functools.partial(_mix_out_kernel, alpha=alpha),
        out_shape=jax.ShapeDtypeStruct((n, d), F32),
        grid=(n // tm,),
        in_specs=[row, brow, brow] + [_const_spec(c.shape) for c in consts],
        out_specs=row,
        compiler_params=pltpu.CompilerParams(
            dimension_semantics=("parallel",), vmem_limit_bytes=VMEM_LIMIT),
        name="mix_out_ln",
    )(h2d, ysb2d, ygdn2d, *consts)


def _pick_tile(n, want):
    t = min(want, n)
    while n % t:
        t //= 2
    return t


def kernel(x, p, ffn1_w_in, ffn1_w_out, ln1_g, ln1_b, w_mix_in, b_gate, conv_w, a_log, dt_bias, gdn_norm_w, w_branch_sb, w_branch_gdn, w_mix_out, ln2_g, ln2_b, ffn2_w_in, ffn2_w_out, ln3_g, ln3_b, w_ple_gate, b_ple_gate, w_ple, ln4_g, ln4_b):
    bsz, s, d = x.shape
    depth = ffn1_w_in.shape[0]
    n = bsz * s
    alpha = (2 * depth) ** 0.25
    d_ff = ffn1_w_out.shape[1]
    n_chunks = 2 if d_ff % 256 == 0 else 1
    tm = _pick_tile(n, 512)
    ts_proj = _pick_tile(s, 512)
    ts_gdn = _pick_tile(s, 512)
    tq = _pick_tile(s, 128)

    def row(vec):
        return vec.reshape(1, -1).astype(F32)

    h = x.reshape(n, d)
    for i in range(depth):
        h = _ffn_ln(h, ffn1_w_in[i].astype(BF16), ffn1_w_out[i].astype(BF16),
                    row(ln1_g[i]), row(ln1_b[i]), alpha=alpha, tm=tm, n_chunks=n_chunks)

        w_in = w_mix_in[i]
        c0 = 7 * WIDTH
        w_main = w_in[:, :c0].astype(BF16)
        w_b = w_in[:, c0:c0 + HEADS].astype(BF16)
        w_a = w_in[:, c0 + HEADS:c0 + 2 * HEADS].astype(BF16)
        w_gate = w_in[:, c0 + 2 * HEADS:].astype(BF16)
        sq, skv, gpre, gz, gb, ga, gbt, gat = _inproj(
            h.reshape(bsz, s, d), w_main, w_b, w_a, tm=ts_proj)
        y_sb = _sb_attention(sq, skv, tq=tq, heads_per_step=HEADS)
        y_gdn = _gdn(gpre, gz, gb, ga, gbt, gat, conv_w[i].astype(F32), a_log[i].astype(F32),
                     dt_bias[i].astype(F32), gdn_norm_w[i].astype(F32), ts=ts_gdn)
        h = _mix_out(h, y_sb.reshape(n, WIDTH), y_gdn.reshape(n, WIDTH), w_gate, row(b_gate[i]),
                     w_branch_sb[i].astype(BF16), w_branch_gdn[i].astype(BF16),
                     w_mix_out[i].astype(BF16), row(ln2_g[i]), row(ln2_b[i]), alpha=alpha, tm=tm)
        h = _ffn_ln_ple(h, p[i].reshape(n, -1), ffn2_w_in[i].astype(BF16), ffn2_w_out[i].astype(BF16),
                        row(ln3_g[i]), row(ln3_b[i]), w_ple_gate[i].astype(BF16), row(b_ple_gate[i]),
                        w_ple[i].astype(BF16), row(ln4_g[i]), row(ln4_b[i]),
                        alpha=alpha, tm=tm, n_chunks=n_chunks)
    return h.reshape(bsz, s, d)
```
